```python
import math
import jax, jax.numpy as jnp
from jax import lax
import numpy as np

D_MODEL = 2048
BATCH = 8
SEQ = 4096
DEPTH = 4

N_MIXERS = 2
N_HEADS = 16
HEAD_DIM = D_MODEL // N_HEADS
D_INNER = N_HEADS * HEAD_DIM
Q_RANK = 512
KV_RANK = 256
IDX_HEADS = 16
IDX_DIM = 64
TOPK_MAX = 256
N_BUCKETS = 32
MAX_DISTANCE = 128
BLOCK_Q = 128
EPS = 1e-6
A_IN = Q_RANK + KV_RANK + IDX_DIM + IDX_HEADS + D_INNER
B_IN = 3 * D_INNER + N_HEADS + D_INNER

kernel_name = "hybrid_dsa_fox_gated_trunk"


def rmsnorm(x, g):
    xf = x.astype(jnp.float32)
    y = xf * lax.rsqrt(jnp.mean(xf * xf, axis=-1, keepdims=True) + EPS)
    return (y * g.astype(jnp.float32)).astype(x.dtype)


def t5_bucket(dist):
    max_exact = N_BUCKETS // 2
    d = jnp.maximum(dist, 0)
    df = jnp.maximum(d, 1).astype(jnp.float32)
    large = max_exact + (jnp.log(df / max_exact) / math.log(MAX_DISTANCE / max_exact)
                         * (N_BUCKETS - max_exact)).astype(jnp.int32)
    large = jnp.minimum(large, N_BUCKETS - 1)
    return jnp.where(d < max_exact, d, large)


def to_blocks(a):
    b, s = a.shape[:2]
    return jnp.swapaxes(a.reshape(b, s // BLOCK_Q, BLOCK_Q, *a.shape[2:]), 0, 1)


def from_blocks(a):
    nb, b, q = a.shape[:3]
    return jnp.swapaxes(a, 0, 1).reshape(b, nb * q, *a.shape[3:])


def dsa_mixer(h, w_in, q_norm, kv_norm, w_q_up, w_uk, w_uv, w_iq, w_out, rel_bias):
    b, s, _ = h.shape
    topk = min(TOPK_MAX, s // 4)
    proj = h @ w_in
    o1 = Q_RANK
    o2 = o1 + KV_RANK
    o3 = o2 + IDX_DIM
    o4 = o3 + IDX_HEADS
    cq = rmsnorm(proj[..., :o1], q_norm)
    c_kv = rmsnorm(proj[..., o1:o2], kv_norm)
    ik = proj[..., o2:o3]
    iw = proj[..., o3:o4] * (IDX_HEADS ** -0.5 * IDX_DIM ** -0.5)
    gate = proj[..., o4:]
    q = (cq @ w_q_up).reshape(b, s, N_HEADS, HEAD_DIM)
    q_abs = jnp.einsum('bshd,rhd->bshr', q, w_uk) * (HEAD_DIM ** -0.5)
    iq = (cq @ w_iq).reshape(b, s, IDX_HEADS, IDX_DIM)
    pos = jnp.arange(s, dtype=jnp.int32)

    def block(args):
        qa, iqb, iwb, tq = args
        dots = jnp.einsum('bthd,bsd->bths', iqb, ik)
        score = jnp.einsum('bths,bth->bts', jax.nn.relu(dots), iwb).astype(jnp.float32)
        causal = pos[None, :] <= tq[:, None]
        score = jnp.where(causal[None], score, -jnp.inf)
        _, idx = lax.top_k(score, topk)
        c_sel = jax.vmap(lambda c, i: c[i])(c_kv, idx)
        dist = tq[None, :, None] - idx
        valid = (dist >= 0)[:, None]
        bias = jnp.moveaxis(rel_bias[t5_bucket(dist)], -1, 1)
        logits = (jnp.einsum('bthr,btkr->bhtk', qa, c_sel).astype(jnp.float32)
                  + bias.astype(jnp.float32))
        logits = jnp.where(valid, logits, -jnp.inf)
        p = jax.nn.softmax(logits, axis=-1).astype(c_sel.dtype)
        return jnp.einsum('bhtk,btkr->bthr', p, c_sel)

    o_lat = from_blocks(lax.map(block, (to_blocks(q_abs), to_blocks(iq), to_blocks(iw),
                                        pos.reshape(-1, BLOCK_Q))))
    o = jnp.einsum('bshr,rhd->bshd', o_lat, w_uv).reshape(b, s, D_INNER)
    return (o * jax.nn.silu(gate)) @ w_out


def fox_mixer(h, w_in, f_bias, w_out):
    b, s, _ = h.shape
    proj = h @ w_in
    q = proj[..., :D_INNER].reshape(b, s, N_HEADS, HEAD_DIM) * (HEAD_DIM ** -0.5)
    k = proj[..., D_INNER:2 * D_INNER].reshape(b, s, N_HEADS, HEAD_DIM)
    v = proj[..., 2 * D_INNER:3 * D_INNER].reshape(b, s, N_HEADS, HEAD_DIM)
    f_pre = proj[..., 3 * D_INNER:3 * D_INNER + N_HEADS]
    gate = proj[..., 3 * D_INNER + N_HEADS:]
    log_f = jax.nn.log_sigmoid((f_pre + f_bias).astype(jnp.float32))
    cum = lax.cumsum(log_f, axis=1)
    cum_keys = jnp.moveaxis(cum, -1, 1)
    pos = jnp.arange(s, dtype=jnp.int32)

    def block(args):
        qb, cb, tq = args
        decay = jnp.moveaxis(cb, -1, 1)[..., None] - cum_keys[:, :, None, :]
        logits = jnp.einsum('bthd,bshd->bhts', qb, k).astype(jnp.float32) + decay
        causal = pos[None, :] <= tq[:, None]
        logits = jnp.where(causal[None, None], logits, -jnp.inf)
        p = jax.nn.softmax(logits, axis=-1).astype(v.dtype)
        return jnp.einsum('bhts,bshd->bthd', p, v)

    o = from_blocks(lax.map(block, (to_blocks(q), to_blocks(cum), pos.reshape(-1, BLOCK_Q))))
    o = o.reshape(b, s, D_INNER)
    return (o * jax.nn.silu(gate)) @ w_out


def setup_inputs(seed: int = 0) -> dict:
    key = jax.random.key(seed)
    ks = jax.random.split(key, 20)
    n_a = (DEPTH + 1) // 2
    n_b = DEPTH // 2
    nrm = lambda k, shape, scale: jax.random.normal(k, shape, jnp.float32) * scale
    return {
        "x": nrm(ks[0], (BATCH, SEQ, D_MODEL), 1.0),
        "norm_g": 1.0 + nrm(ks[1], (DEPTH, D_MODEL), 0.02),
        "final_g": 1.0 + nrm(ks[2], (D_MODEL,), 0.02),
        "rel_bias": nrm(ks[3], (N_BUCKETS, N_HEADS), 0.5),
        "a_w_in": nrm(ks[4], (n_a, D_MODEL, A_IN), D_MODEL ** -0.5),
        "a_q_norm": 1.0 + nrm(ks[5], (n_a, Q_RANK), 0.02),
        "a_kv_norm": 1.0 + nrm(ks[6], (n_a, KV_RANK), 0.02),
        "a_w_q_up": nrm(ks[7], (n_a, Q_RANK, D_INNER), Q_RANK ** -0.5),
        "a_w_uk": nrm(ks[8], (n_a, KV_RANK, N_HEADS, HEAD_DIM), KV_RANK ** -0.5),
        "a_w_uv": nrm(ks[9], (n_a, KV_RANK, N_HEADS, HEAD_DIM), KV_RANK ** -0.5),
        "a_w_iq": nrm(ks[10], (n_a, Q_RANK, IDX_HEADS * IDX_DIM), Q_RANK ** -0.5),
        "a_w_out": nrm(ks[11], (n_a, D_INNER, D_MODEL), D_INNER ** -0.5),
        "b_w_in": nrm(ks[12], (n_b, D_MODEL, B_IN), D_MODEL ** -0.5),
        "b_f_bias": 4.0 + nrm(ks[13], (n_b, N_HEADS), 0.5),
        "b_w_out": nrm(ks[14], (n_b, D_INNER, D_MODEL), D_INNER ** -0.5),
    }


def reference(x, norm_g, final_g, rel_bias, a_w_in, a_q_norm, a_kv_norm, a_w_q_up,
              a_w_uk, a_w_uv, a_w_iq, a_w_out, b_w_in, b_f_bias, b_w_out):
    h = x
    for i in range(DEPTH):
        hn = rmsnorm(h, norm_g[i])
        j = i // N_MIXERS
        if i % N_MIXERS == 0:
            y = dsa_mixer(hn, a_w_in[j], a_q_norm[j], a_kv_norm[j], a_w_q_up[j],
                          a_w_uk[j], a_w_uv[j], a_w_iq[j], a_w_out[j], rel_bias)
        else:
            y = fox_mixer(hn, b_w_in[j], b_f_bias[j], b_w_out[j])
        h = h + y
    return rmsnorm(h, final_g)
```

```python
import functools
import math

import numpy as np
import jax
import jax.numpy as jnp
from jax import lax
from jax.experimental import pallas as pl
from jax.experimental.pallas import tpu as pltpu

N_HEADS = 16
HEAD_DIM = 128
Q_RANK = 512
KV_RANK = 256
IDX_HEADS = 16
IDX_DIM = 64
TOPK_MAX = 256
N_BUCKETS = 32
MAX_DISTANCE = 128
EPS = 1e-6

LOG2E = 1.4426950408889634
MASKED = -1e30
INT_MIN = -2147483648
LANES = 128
VMEM_LIMIT = 48 * 1024 * 1024

DSA_TQ = 128
DSA_TK = 256
FOX_TQ = 512
FOX_TK = 512

_NT = (((1,), (1,)), ((), ()))


def _cparams(*sem):
    return pltpu.CompilerParams(dimension_semantics=sem, vmem_limit_bytes=VMEM_LIMIT)


def _rep(x, n):
    return x if n == 1 else jnp.concatenate([x] * n, axis=1)


def _rmsnorm_kernel(x_ref, g_ref, o_ref):
    x = x_ref[...]
    ms = jnp.mean(x * x, axis=-1, keepdims=True)
    o_ref[...] = (x * lax.rsqrt(ms + EPS) * g_ref[...]).astype(o_ref.dtype)


def _rmsnorm(x2d, g, out_dtype, tm=512):
    m, d = x2d.shape
    return pl.pallas_call(
        _rmsnorm_kernel,
        grid=(m // tm,),
        in_specs=[pl.BlockSpec((tm, d), lambda i: (i, 0)),
                  pl.BlockSpec((1, d), lambda i: (0, 0))],
        out_specs=pl.BlockSpec((tm, d), lambda i: (i, 0)),
        out_shape=jax.ShapeDtypeStruct((m, d), out_dtype),
        compiler_params=_cparams("parallel"),
        name="rmsnorm",
    )(x2d, g.reshape(1, d))


def _matmul_kernel(x_ref, w_ref, cs_ref, o_ref):
    acc = jnp.dot(x_ref[...], w_ref[...], preferred_element_type=jnp.float32)
    o_ref[...] = (acc * cs_ref[...]).astype(o_ref.dtype)


def _matmul(x, w, colscale, out_dtype, tm, tn):
    m, k = x.shape
    n = w.shape[1]
    tm, tn = min(tm, m), min(tn, n)
    return pl.pallas_call(
        _matmul_kernel,
        grid=(n // tn, m // tm),
        in_specs=[pl.BlockSpec((tm, k), lambda j, i: (i, 0)),
                  pl.BlockSpec((k, tn), lambda j, i: (0, j)),
                  pl.BlockSpec((1, tn), lambda j, i: (0, j))],
        out_specs=pl.BlockSpec((tm, tn), lambda j, i: (i, j)),
        out_shape=jax.ShapeDtypeStruct((m, n), out_dtype),
        compiler_params=_cparams("parallel", "parallel"),
        name="matmul",
    )(x, w, colscale.reshape(1, n))


def _outproj_kernel(x_ref, w_ref, h_ref, g_ref, hnew_ref, hn_ref):
    y = jnp.dot(x_ref[...], w_ref[...], preferred_element_type=jnp.float32)
    hnew = h_ref[...] + y
    hnew_ref[...] = hnew
    ms = jnp.mean(hnew * hnew, axis=-1, keepdims=True)
    hn_ref[...] = (hnew * lax.rsqrt(ms + EPS) * g_ref[...]).astype(hn_ref.dtype)


def _outproj(x, w, h, g_next, hn_dtype, tm=256):
    m, k = x.shape
    d = w.shape[1]
    return pl.pallas_call(
        _outproj_kernel,
        grid=(m // tm,),
        in_specs=[pl.BlockSpec((tm, k), lambda i: (i, 0)),
                  pl.BlockSpec((k, d), lambda i: (0, 0)),
                  pl.BlockSpec((tm, d), lambda i: (i, 0)),
                  pl.BlockSpec((1, d), lambda i: (0, 0))],
        out_specs=[pl.BlockSpec((tm, d), lambda i: (i, 0)),
                   pl.BlockSpec((tm, d), lambda i: (i, 0))],
        out_shape=[jax.ShapeDtypeStruct((m, d), jnp.float32),
                   jax.ShapeDtypeStruct((m, d), hn_dtype)],
        compiler_params=_cparams("parallel"),
        name="outproj",
    )(x, w, h, g_next.reshape(1, d))


def _dsa_latent_kernel(sm_ref, qn_ref, kvn_ref, wq_ref, wukt_ref, wiq_ref,
                       ckv_ref, ik_ref, iw_ref, qa_ref, iq_ref):
    sm = sm_ref[0]
    o1, o2, o3, o4 = Q_RANK, Q_RANK + KV_RANK, Q_RANK + KV_RANK + IDX_DIM, \
        Q_RANK + KV_RANK + IDX_DIM + IDX_HEADS

    def rms(v, g):
        ms = jnp.mean(v * v, axis=-1, keepdims=True)
        return v * lax.rsqrt(ms + EPS) * g

    cq = rms(sm[:, :o1], qn_ref[...]).astype(jnp.bfloat16)
    ckv_ref[0] = rms(sm[:, o1:o2], kvn_ref[...]).astype(jnp.bfloat16)
    ik_ref[0] = sm[:, o2:o3].astype(jnp.bfloat16)
    iw_ref[0] = sm[:, o3:o4] * (IDX_HEADS ** -0.5 * IDX_DIM ** -0.5)
    q = jnp.dot(cq, wq_ref[...], preferred_element_type=jnp.float32).astype(jnp.bfloat16)
    iq = jnp.dot(cq, wiq_ref[...], preferred_element_type=jnp.float32).astype(jnp.bfloat16)
    for h in range(N_HEADS):
        qh = q[:, h * HEAD_DIM:(h + 1) * HEAD_DIM]
        qa = jnp.dot(qh, wukt_ref[h], preferred_element_type=jnp.float32)
        qa_ref[0, h] = (qa * (HEAD_DIM ** -0.5 * LOG2E)).astype(jnp.bfloat16)
    for h in range(IDX_HEADS):
        iq_ref[0, h] = iq[:, h * IDX_DIM:(h + 1) * IDX_DIM]


def _dsa_latents(small, q_norm, kv_norm, wq, wukt, wiq, tm=512):
    b, s, ns = small.shape
    full = lambda *shape: pl.BlockSpec(shape, lambda bi, i: (0,) * len(shape))
    return pl.pallas_call(
        _dsa_latent_kernel,
        grid=(b, s // tm),
        in_specs=[pl.BlockSpec((1, tm, ns), lambda bi, i: (bi, i, 0)),
                  full(1, Q_RANK), full(1, KV_RANK),
                  full(Q_RANK, N_HEADS * HEAD_DIM),
                  full(N_HEADS, HEAD_DIM, KV_RANK),
                  full(Q_RANK, IDX_HEADS * IDX_DIM)],
        out_specs=[pl.BlockSpec((1, tm, KV_RANK), lambda bi, i: (bi, i, 0)),
                   pl.BlockSpec((1, tm, IDX_DIM), lambda bi, i: (bi, i, 0)),
                   pl.BlockSpec((1, tm, IDX_HEADS), lambda bi, i: (bi, i, 0)),
                   pl.BlockSpec((1, N_HEADS, tm, KV_RANK), lambda bi, i: (bi, 0, i, 0)),
                   pl.BlockSpec((1, IDX_HEADS, tm, IDX_DIM), lambda bi, i: (bi, 0, i, 0))],
        out_shape=[jax.ShapeDtypeStruct((b, s, KV_RANK), jnp.bfloat16),
                   jax.ShapeDtypeStruct((b, s, IDX_DIM), jnp.bfloat16),
                   jax.ShapeDtypeStruct((b, s, IDX_HEADS), jnp.float32),
                   jax.ShapeDtypeStruct((b, N_HEADS, s, KV_RANK), jnp.bfloat16),
                   jax.ShapeDtypeStruct((b, IDX_HEADS, s, IDX_DIM), jnp.bfloat16)],
        compiler_params=_cparams("parallel", "parallel"),
        name="dsa_latents",
    )(small, q_norm.reshape(1, -1), kv_norm.reshape(1, -1), wq, wukt, wiq)


def _t5_bucket_np(dist):
    max_exact = N_BUCKETS // 2
    d = np.maximum(dist, 0)
    df = np.maximum(d, 1).astype(np.float32)
    large = max_exact + (np.log(df / np.float32(max_exact)) / np.float32(math.log(MAX_DISTANCE / max_exact))
                         * np.float32(N_BUCKETS - max_exact)).astype(np.int32)
    large = np.minimum(large, N_BUCKETS - 1)
    return np.where(d < max_exact, d, large).astype(np.int32)


def _bias_table_kernel(bucket_ref, rb_ref, o_ref):
    h = pl.program_id(1)
    bk = bucket_ref[0]
    far = rb_ref[N_BUCKETS - 1, h]
    acc = jnp.zeros(bk.shape, jnp.float32)
    for b in range(N_BUCKETS - 1):
        acc = jnp.where(bk == b, rb_ref[b, h] - far, acc)
    o_ref[0, 0] = acc * LOG2E


def _bias_tables(rel_bias, tq, tk):
    n_slots = 2 * tk // tq
    r = np.arange(tq)[:, None]
    c = np.arange(tk)[None, :]
    buckets = np.stack([_t5_bucket_np(sl * tq + r - c) for sl in range(n_slots)])
    return pl.pallas_call(
        _bias_table_kernel,
        grid=(n_slots, N_HEADS),
        in_specs=[pl.BlockSpec((1, tq, tk), lambda sl, h: (sl, 0, 0)),
                  pl.BlockSpec(memory_space=pltpu.SMEM)],
        out_specs=pl.BlockSpec((1, 1, tq, tk), lambda sl, h: (sl, h, 0, 0)),
        out_shape=jax.ShapeDtypeStruct((n_slots, N_HEADS, tq, tk), jnp.float32),
        compiler_params=_cparams("parallel", "parallel"),
        name="t5_bias_tables",
    )(jnp.asarray(buckets), rel_bias)


def _dsa_attn_kernel(iq_ref, iw_ref, ik_ref, ckv_ref, qa_ref, gate_ref, wuv_ref, tbl_ref,
                     o_ref,
                     key_ref, mb_ref, s_ref, p_ref, acc_ref, m_ref, l_ref, al_ref, wb_ref,
                     *, topk):
    tq, tk = DSA_TQ, DSA_TK
    nrep = tk // LANES
    i = pl.program_id(1)
    t0 = i * tq
    kd = t0 // tk
    n_kt = kd + 1
    par = i % (tk // tq)

    iw = iw_ref[0]
    for h in range(IDX_HEADS):
        wb_ref[h] = jnp.broadcast_to(iw[:, h:h + 1], (tq, LANES))

    def score_tile(kt, carry):
        k0 = pl.multiple_of(kt * tk, tk)
        ikt = ik_ref[0, pl.ds(k0, tk), :]
        acc = jnp.zeros((tq, tk), jnp.float32)
        for h in range(IDX_HEADS):
            d = lax.dot_general(iq_ref[0, h], ikt, _NT, preferred_element_type=jnp.float32)
            acc = acc + jnp.maximum(d, 0.0) * _rep(wb_ref[h], nrep)
        bits = lax.bitcast_convert_type(acc, jnp.int32)
        key = jnp.where(bits < 0, bits ^ jnp.int32(0x7FFFFFFF), bits)
        row = t0 + lax.broadcasted_iota(jnp.int32, (tq, tk), 0)
        col = k0 + lax.broadcasted_iota(jnp.int32, (tq, tk), 1)
        key_ref[:, pl.ds(k0, tk)] = jnp.where(col <= row, key, jnp.int32(INT_MIN))
        return carry

    lax.fori_loop(0, n_kt, score_tile, 0)

    def count_ge(cand):
        cb = jnp.broadcast_to(cand, (tq, LANES))

        def body(kt, c):
            k0 = pl.multiple_of(kt * tk, tk)
            kk = key_ref[:, pl.ds(k0, tk)]
            for j in range(nrep):
                c = c + jnp.where(kk[:, j * LANES:(j + 1) * LANES] >= cb, 1.0, 0.0)
            return c

        c = lax.fori_loop(0, n_kt, body, jnp.zeros((tq, LANES), jnp.float32))
        return jnp.sum(c, axis=1, keepdims=True)

    kf = jnp.float32(topk)
    zero = jnp.zeros((tq, 1), jnp.int32)
    ans0 = jnp.where(count_ge(zero) >= kf, zero, jnp.int32(INT_MIN))

    def search(it, ans):
        cand = ans + jnp.left_shift(jnp.int32(1), jnp.int32(30) - it)
        return jnp.where(count_ge(cand) >= kf, cand, ans)

    thr = lax.fori_loop(0, 31, search, ans0)
    thr = jnp.maximum(thr, jnp.int32(INT_MIN + 1))
    thr_b = jnp.broadcast_to(thr, (tq, LANES))

    def mask_tile(kt, carry):
        k0 = pl.multiple_of(kt * tk, tk)
        kk = key_ref[:, pl.ds(k0, tk)]
        mb_ref[:, pl.ds(k0, tk)] = jnp.where(kk >= _rep(thr_b, nrep), 0.0, MASKED)
        return carry

    lax.fori_loop(0, n_kt, mask_tile, 0)

    m_ref[...] = jnp.full(m_ref.shape, MASKED, jnp.float32)
    l_ref[...] = jnp.zeros(l_ref.shape, jnp.float32)
    acc_ref[...] = jnp.zeros(acc_ref.shape, jnp.float32)
    qa = qa_ref[0].reshape(N_HEADS * tq, KV_RANK)

    def attn_tile(kt, slot):
        k0 = pl.multiple_of(kt * tk, tk)
        ck = ckv_ref[0, pl.ds(k0, tk), :]
        s_ref[...] = lax.dot_general(qa, ck, _NT, preferred_element_type=jnp.float32)
        mbt = mb_ref[:, pl.ds(k0, tk)]

        def head(h, carry):
            rows = pl.ds(pl.multiple_of(h * tq, tq), tq)
            s = s_ref[rows, :] + mbt
            if slot is not None:
                s = s + tbl_ref[slot, h]
            m_prev = m_ref[rows, :]
            m_new = jnp.maximum(m_prev, jnp.max(s, axis=1, keepdims=True))
            alpha = jnp.exp2(m_prev - m_new)
            p = jnp.exp2(s - _rep(m_new, nrep))
            l_ref[rows, :] = alpha * l_ref[rows, :] + jnp.sum(p, axis=1, keepdims=True)
            m_ref[rows, :] = m_new
            al_ref[rows, :] = alpha
            p_ref[rows, :] = p.astype(jnp.bfloat16)
            return carry

        lax.fori_loop(0, N_HEADS, head, 0)
        pv = jnp.dot(p_ref[...], ck, preferred_element_type=jnp.float32)
        acc_ref[...] = acc_ref[...] * _rep(al_ref[...], KV_RANK // LANES) + pv

    def far_tile(kt, carry):
        attn_tile(kt, None)
        return carry

    lax.fori_loop(0, jnp.maximum(kd - 1, 0), far_tile, 0)

    @pl.when(kd >= 1)
    def _():
        attn_tile(kd - 1, par + tk // tq)

    attn_tile(kd, par)

    for h in range(N_HEADS):
        rows = slice(h * tq, (h + 1) * tq)
        o_lat = (acc_ref[rows, :] / _rep(l_ref[rows, :], KV_RANK // LANES)).astype(jnp.bfloat16)
        o = jnp.dot(o_lat, wuv_ref[h], preferred_element_type=jnp.float32)
        g = gate_ref[0, :, h * HEAD_DIM:(h + 1) * HEAD_DIM].astype(jnp.float32)
        o_ref[0, :, h * HEAD_DIM:(h + 1) * HEAD_DIM] = (o * (g * jax.nn.sigmoid(g))).astype(o_ref.dtype)


def _dsa_attention(iq, iw, ik, ckv, qa, gate, wuv, tbl, topk):
    b, s, _ = ckv.shape
    tq, tk = DSA_TQ, DSA_TK
    d_inner = N_HEADS * HEAD_DIM
    const = lambda *shape: pl.BlockSpec(shape, lambda bi, i: (0,) * len(shape))
    return pl.pallas_call(
        functools.partial(_dsa_attn_kernel, topk=topk),
        grid=(b, s // tq),
        in_specs=[pl.BlockSpec((1, IDX_HEADS, tq, IDX_DIM), lambda bi, i: (bi, 0, i, 0)),
                  pl.BlockSpec((1, tq, IDX_HEADS), lambda bi, i: (bi, i, 0)),
                  pl.BlockSpec((1, s, IDX_DIM), lambda bi, i: (bi, 0, 0)),
                  pl.BlockSpec((1, s, KV_RANK), lambda bi, i: (bi, 0, 0)),
                  pl.BlockSpec((1, N_HEADS, tq, KV_RANK), lambda bi, i: (bi, 0, i, 0)),
                  pl.BlockSpec((1, tq, d_inner), lambda bi, i: (bi, i, 0)),
                  const(N_HEADS, KV_RANK, HEAD_DIM),
                  const(*tbl.shape)],
        out_specs=pl.BlockSpec((1, tq, d_inner), lambda bi, i: (bi, i, 0)),
        out_shape=jax.ShapeDtypeStruct((b, s, d_inner), jnp.bfloat16),
        scratch_shapes=[pltpu.VMEM((tq, s), jnp.int32),
                        pltpu.VMEM((tq, s), jnp.float32),
                        pltpu.VMEM((N_HEADS * tq, tk), jnp.float32),
                        pltpu.VMEM((N_HEADS * tq, tk), jnp.bfloat16),
                        pltpu.VMEM((N_HEADS * tq, KV_RANK), jnp.float32),
                        pltpu.VMEM((N_HEADS * tq, LANES), jnp.float32),
                        pltpu.VMEM((N_HEADS * tq, LANES), jnp.float32),
                        pltpu.VMEM((N_HEADS * tq, LANES), jnp.float32),
                        pltpu.VMEM((IDX_HEADS, tq, LANES), jnp.float32)],
        compiler_params=_cparams("parallel", "arbitrary"),
        name="dsa_attention",
    )(iq, iw, ik, ckv, qa, gate, wuv, tbl)


def _fox_decay_kernel(f_ref, fb_ref, o_ref, *, chunk):
    x = f_ref[0] + fb_ref[...]
    lf = jnp.minimum(x, 0.0) - jnp.log(1.0 + jnp.exp(-jnp.abs(x)))
    s = x.shape[1]
    r = lax.broadcasted_iota(jnp.int32, (chunk, chunk), 0)
    c = lax.broadcasted_iota(jnp.int32, (chunk, chunk), 1)
    tri = jnp.where(r <= c, 1.0, 0.0).astype(jnp.bfloat16)
    carry = jnp.zeros((x.shape[0], 1), jnp.float32)
    for j in range(s // chunk):
        blk = lf[:, j * chunk:(j + 1) * chunk]
        hi = blk.astype(jnp.bfloat16)
        r1 = blk - hi.astype(jnp.float32)
        mid = r1.astype(jnp.bfloat16)
        lo = (r1 - mid.astype(jnp.float32)).astype(jnp.bfloat16)
        cs = (jnp.dot(hi, tri, preferred_element_type=jnp.float32)
              + jnp.dot(mid, tri, preferred_element_type=jnp.float32)
              + jnp.dot(lo, tri, preferred_element_type=jnp.float32)) + carry
        o_ref[0, :, j * chunk:(j + 1) * chunk] = cs * (-LOG2E)
        carry = cs[:, chunk - 1:chunk]


def _fox_decay(f_t, f_bias, chunk=512):
    b, nh, s = f_t.shape
    chunk = min(chunk, s)
    return pl.pallas_call(
        functools.partial(_fox_decay_kernel, chunk=chunk),
        grid=(b,),
        in_specs=[pl.BlockSpec((1, nh, s), lambda bi: (bi, 0, 0)),
                  pl.BlockSpec((nh, 1), lambda bi: (0, 0))],
        out_specs=pl.BlockSpec((1, nh, s), lambda bi: (bi, 0, 0)),
        out_shape=jax.ShapeDtypeStruct((b, nh, s), jnp.float32),
        compiler_params=_cparams("parallel"),
        name="fox_decay",
    )(f_t, f_bias.reshape(nh, 1))


def _fox_attn_kernel(q_ref, k_ref, v_ref, dk_ref, gate_ref, o_ref, m_ref, l_ref, acc_ref, *, tq, tk):
    nrep = tk // LANES
    i = pl.program_id(2)
    q = q_ref[0]
    m_ref[...] = jnp.full(m_ref.shape, MASKED, jnp.float32)
    l_ref[...] = jnp.zeros(l_ref.shape, jnp.float32)
    acc_ref[...] = jnp.zeros(acc_ref.shape, jnp.float32)

    def tile(kt, masked):
        k0 = pl.multiple_of(kt * tk, tk)
        k = k_ref[0, pl.ds(k0, tk), :]
        v = v_ref[0, pl.ds(k0, tk), :]
        s = lax.dot_general(q, k, _NT, preferred_element_type=jnp.float32)
        s = s + dk_ref[0, 0, :, pl.ds(k0, tk)]
        if masked:
            row = lax.broadcasted_iota(jnp.int32, (tq, tk), 0)
            col = lax.broadcasted_iota(jnp.int32, (tq, tk), 1)
            s = jnp.where(col <= row, s, MASKED)
        m_prev = m_ref[...]
        m_new = jnp.maximum(m_prev, jnp.max(s, axis=1, keepdims=True))
        alpha = jnp.exp2(m_prev - m_new)
        p = jnp.exp2(s - _rep(m_new, nrep))
        l_ref[...] = alpha * l_ref[...] + jnp.sum(p, axis=1, keepdims=True)
        m_ref[...] = m_new
        pv = jnp.dot(p.astype(jnp.bfloat16), v, preferred_element_type=jnp.float32)
        acc_ref[...] = acc_ref[...] * alpha + pv

    def full_tile(kt, carry):
        tile(kt, False)
        return carry

    lax.fori_loop(0, i, full_tile, 0)
    tile(i, True)

    g = gate_ref[0].astype(jnp.float32)
    o = acc_ref[...] / l_ref[...]
    o_ref[0] = (o * (g * jax.nn.sigmoid(g))).astype(o_ref.dtype)


def _fox_attention(proj, dk, d_inner):
    b, s, _ = proj.shape
    tq = tk = min(FOX_TQ, s)
    nh = d_inner // HEAD_DIM
    return pl.pallas_call(
        functools.partial(_fox_attn_kernel, tq=tq, tk=tk),
        grid=(b, nh, s // tq),
        in_specs=[pl.BlockSpec((1, tq, HEAD_DIM), lambda bi, h, i: (bi, i, h)),
                  pl.BlockSpec((1, s, HEAD_DIM), lambda bi, h, i: (bi, 0, nh + h)),
                  pl.BlockSpec((1, s, HEAD_DIM), lambda bi, h, i: (bi, 0, 2 * nh + h)),
                  pl.BlockSpec((1, 1, 1, s), lambda bi, h, i: (bi, h, 0, 0)),
                  pl.BlockSpec((1, tq, HEAD_DIM), lambda bi, h, i: (bi, i, 3 * nh + h))],
        out_specs=pl.BlockSpec((1, tq, HEAD_DIM), lambda bi, h, i: (bi, i, h)),
        out_shape=jax.ShapeDtypeStruct((b, s, d_inner), jnp.bfloat16),
        scratch_shapes=[pltpu.VMEM((tq, HEAD_DIM), jnp.float32),
                        pltpu.VMEM((tq, HEAD_DIM), jnp.float32),
                        pltpu.VMEM((tq, HEAD_DIM), jnp.float32)],
        compiler_params=_cparams("parallel", "parallel", "arbitrary"),
        name="fox_attention",
    )(proj, proj, proj, dk, proj)


def _dsa_layer(hn, h, g_next, hn_dtype, b, s, w_in, q_norm, kv_norm, w_q_up, w_uk, w_uv, w_iq, w_out, tbl):
    bf = jnp.bfloat16
    d_inner = N_HEADS * HEAD_DIM
    n_small = Q_RANK + KV_RANK + IDX_DIM + IDX_HEADS
    n_pad = -(-n_small // LANES) * LANES
    w_small = jnp.pad(w_in[:, :n_small], ((0, 0), (0, n_pad - n_small))).astype(bf)
    w_gate = w_in[:, n_small:].astype(bf)
    small = _matmul(hn, w_small, jnp.ones((n_pad,), jnp.float32), jnp.float32, 1024, n_pad)
    gate = _matmul(hn, w_gate, jnp.ones((d_inner,), jnp.float32), bf, 1024, 1024)
    wukt = jnp.transpose(w_uk, (1, 2, 0)).astype(bf)
    wuv = jnp.transpose(w_uv, (1, 0, 2)).astype(bf)
    ckv, ik, iw, qa, iq = _dsa_latents(small.reshape(b, s, n_pad), q_norm, kv_norm,
                                       w_q_up.astype(bf), wukt, w_iq.astype(bf),
                                       tm=min(512, s))
    topk = min(TOPK_MAX, s // 4)
    og = _dsa_attention(iq, iw, ik, ckv, qa, gate.reshape(b, s, d_inner), wuv, tbl, topk)
    return _outproj(og.reshape(b * s, d_inner), w_out.astype(bf), h, g_next, hn_dtype)


def _fox_layer(hn, h, g_next, hn_dtype, b, s, w_in, f_bias, w_out):
    bf = jnp.bfloat16
    d_inner = N_HEADS * HEAD_DIM
    w_qkvg = jnp.concatenate([w_in[:, :3 * d_inner], w_in[:, 3 * d_inner + N_HEADS:]], axis=1).astype(bf)
    w_f = jnp.pad(w_in[:, 3 * d_inner:3 * d_inner + N_HEADS], ((0, 0), (0, LANES - N_HEADS))).astype(bf)
    colscale = jnp.concatenate([jnp.full((d_inner,), HEAD_DIM ** -0.5 * LOG2E, jnp.float32),
                                jnp.ones((3 * d_inner,), jnp.float32)])
    proj = _matmul(hn, w_qkvg, colscale, bf, 1024, 1024)
    f_pre = _matmul(hn, w_f, jnp.ones((LANES,), jnp.float32), jnp.float32, 1024, LANES)
    f_t = jnp.transpose(f_pre.reshape(b, s, LANES)[:, :, :N_HEADS], (0, 2, 1))
    dk = _fox_decay(f_t, f_bias).reshape(b, N_HEADS, 1, s)
    og = _fox_attention(proj.reshape(b, s, 4 * d_inner), dk, d_inner)
    return _outproj(og.reshape(b * s, d_inner), w_out.astype(bf), h, g_next, hn_dtype)


def kernel(x, norm_g, final_g, rel_bias, a_w_in, a_q_norm, a_kv_norm, a_w_q_up, a_w_uk, a_w_uv,
           a_w_iq, a_w_out, b_w_in, b_f_bias, b_w_out):
    b, s, d = x.shape
    depth = norm_g.shape[0]
    tbl = _bias_tables(rel_bias, DSA_TQ, DSA_TK)
    h = x.reshape(b * s, d)
    hn = _rmsnorm(h, norm_g[0], jnp.bfloat16)
    for i in range(depth):
        last = i == depth - 1
        g_next = final_g if last else norm_g[i + 1]
        hn_dtype = x.dtype if last else jnp.bfloat16
        j = i // 2
        if i % 2 == 0:
            h, hn = _dsa_layer(hn, h, g_next, hn_dtype, b, s, a_w_in[j], a_q_norm[j], a_kv_norm[j],
                               a_w_q_up[j], a_w_uk[j], a_w_uv[j], a_w_iq[j], a_w_out[j], tbl)
        else:
            h, hn = _fox_layer(hn, h, g_next, hn_dtype, b, s, b_w_in[j], b_f_bias[j], b_w_out[j])
    return hn.reshape(b, s, d)
```

```python
import functools
import math

import numpy as np
import jax
import jax.numpy as jnp
from jax import lax
from jax.experimental import pallas as pl
from jax.experimental.pallas import tpu as pltpu

N_HEADS = 16
HEAD_DIM = 128
Q_RANK = 512
KV_RANK = 256
IDX_HEADS = 16
IDX_DIM = 64
TOPK_MAX = 256
N_BUCKETS = 32
MAX_DISTANCE = 128
EPS = 1e-6

LOG2E = 1.4426950408889634
MASKED = -1e30
INT_MIN = -2147483648
LANES = 128
VMEM_LIMIT = 48 * 1024 * 1024

DSA_TQ = 128
DSA_TK = 256
DSA_HG = 4
FOX_TQ = 512
FOX_TK = 512
FOX_HG = 2

_NT = (((1,), (1,)), ((), ()))


def _cparams(*sem):
    return pltpu.CompilerParams(dimension_semantics=sem, vmem_limit_bytes=VMEM_LIMIT)


def _rep(x, n):
    return x if n == 1 else jnp.concatenate([x] * n, axis=1)


def _rmsnorm_kernel(x_ref, g_ref, o_ref):
    x = x_ref[...]
    ms = jnp.mean(x * x, axis=-1, keepdims=True)
    o_ref[...] = (x * lax.rsqrt(ms + EPS) * g_ref[...]).astype(o_ref.dtype)


def _rmsnorm(x2d, g, out_dtype, tm=512):
    m, d = x2d.shape
    return pl.pallas_call(
        _rmsnorm_kernel,
        grid=(m // tm,),
        in_specs=[pl.BlockSpec((tm, d), lambda i: (i, 0)),
                  pl.BlockSpec((1, d), lambda i: (0, 0))],
        out_specs=pl.BlockSpec((tm, d), lambda i: (i, 0)),
        out_shape=jax.ShapeDtypeStruct((m, d), out_dtype),
        compiler_params=_cparams("parallel"),
        name="rmsnorm",
    )(x2d, g.reshape(1, d))


def _matmul_kernel(x_ref, w_ref, cs_ref, o_ref):
    acc = jnp.dot(x_ref[...], w_ref[...], preferred_element_type=jnp.float32)
    o_ref[...] = (acc * cs_ref[...]).astype(o_ref.dtype)


def _matmul(x, w, colscale, out_dtype, tm, tn):
    m, k = x.shape
    n = w.shape[1]
    tm, tn = min(tm, m), min(tn, n)
    return pl.pallas_call(
        _matmul_kernel,
        grid=(n // tn, m // tm),
        in_specs=[pl.BlockSpec((tm, k), lambda j, i: (i, 0)),
                  pl.BlockSpec((k, tn), lambda j, i: (0, j)),
                  pl.BlockSpec((1, tn), lambda j, i: (0, j))],
        out_specs=pl.BlockSpec((tm, tn), lambda j, i: (i, j)),
        out_shape=jax.ShapeDtypeStruct((m, n), out_dtype),
        compiler_params=_cparams("parallel", "parallel"),
        name="matmul",
    )(x, w, colscale.reshape(1, n))


def _outproj_kernel(x_ref, w_ref, h_ref, g_ref, hnew_ref, hn_ref):
    y = jnp.dot(x_ref[...], w_ref[...], preferred_element_type=jnp.float32)
    hnew = h_ref[...] + y
    hnew_ref[...] = hnew
    ms = jnp.mean(hnew * hnew, axis=-1, keepdims=True)
    hn_ref[...] = (hnew * lax.rsqrt(ms + EPS) * g_ref[...]).astype(hn_ref.dtype)


def _outproj(x, w, h, g_next, hn_dtype, tm=256):
    m, k = x.shape
    d = w.shape[1]
    return pl.pallas_call(
        _outproj_kernel,
        grid=(m // tm,),
        in_specs=[pl.BlockSpec((tm, k), lambda i: (i, 0)),
                  pl.BlockSpec((k, d), lambda i: (0, 0)),
                  pl.BlockSpec((tm, d), lambda i: (i, 0)),
                  pl.BlockSpec((1, d), lambda i: (0, 0))],
        out_specs=[pl.BlockSpec((tm, d), lambda i: (i, 0)),
                   pl.BlockSpec((tm, d), lambda i: (i, 0))],
        out_shape=[jax.ShapeDtypeStruct((m, d), jnp.float32),
                   jax.ShapeDtypeStruct((m, d), hn_dtype)],
        compiler_params=_cparams("parallel"),
        name="outproj",
    )(x, w, h, g_next.reshape(1, d))


def _dsa_latent_kernel(sm_ref, qn_ref, kvn_ref, wq_ref, wukt_ref, wiq_ref,
                       ckv_ref, ik_ref, iw_ref, qa_ref, iq_ref):
    sm = sm_ref[0]
    o1, o2, o3, o4 = Q_RANK, Q_RANK + KV_RANK, Q_RANK + KV_RANK + IDX_DIM, \
        Q_RANK + KV_RANK + IDX_DIM + IDX_HEADS

    def rms(v, g):
        ms = jnp.mean(v * v, axis=-1, keepdims=True)
        return v * lax.rsqrt(ms + EPS) * g

    cq = rms(sm[:, :o1], qn_ref[...]).astype(jnp.bfloat16)
    ckv_ref[0] = rms(sm[:, o1:o2], kvn_ref[...]).astype(jnp.bfloat16)
    ik_ref[0] = sm[:, o2:o3].astype(jnp.bfloat16)
    iw_ref[0] = sm[:, o3:o4] * (IDX_HEADS ** -0.5 * IDX_DIM ** -0.5)
    q = jnp.dot(cq, wq_ref[...], preferred_element_type=jnp.float32).astype(jnp.bfloat16)
    iq = jnp.dot(cq, wiq_ref[...], preferred_element_type=jnp.float32).astype(jnp.bfloat16)
    for h in range(N_HEADS):
        qh = q[:, h * HEAD_DIM:(h + 1) * HEAD_DIM]
        qa = jnp.dot(qh, wukt_ref[h], preferred_element_type=jnp.float32)
        qa_ref[0, h] = (qa * (HEAD_DIM ** -0.5 * LOG2E)).astype(jnp.bfloat16)
    for h in range(IDX_HEADS):
        iq_ref[0, h] = iq[:, h * IDX_DIM:(h + 1) * IDX_DIM]


def _dsa_latents(small, q_norm, kv_norm, wq, wukt, wiq, tm=512):
    b, s, ns = small.shape
    full = lambda *shape: pl.BlockSpec(shape, lambda bi, i: (0,) * len(shape))
    return pl.pallas_call(
        _dsa_latent_kernel,
        grid=(b, s // tm),
        in_specs=[pl.BlockSpec((1, tm, ns), lambda bi, i: (bi, i, 0)),
                  full(1, Q_RANK), full(1, KV_RANK),
                  full(Q_RANK, N_HEADS * HEAD_DIM),
                  full(N_HEADS, HEAD_DIM, KV_RANK),
                  full(Q_RANK, IDX_HEADS * IDX_DIM)],
        out_specs=[pl.BlockSpec((1, tm, KV_RANK), lambda bi, i: (bi, i, 0)),
                   pl.BlockSpec((1, tm, IDX_DIM), lambda bi, i: (bi, i, 0)),
                   pl.BlockSpec((1, tm, IDX_HEADS), lambda bi, i: (bi, i, 0)),
                   pl.BlockSpec((1, N_HEADS, tm, KV_RANK), lambda bi, i: (bi, 0, i, 0)),
                   pl.BlockSpec((1, IDX_HEADS, tm, IDX_DIM), lambda bi, i: (bi, 0, i, 0))],
        out_shape=[jax.ShapeDtypeStruct((b, s, KV_RANK), jnp.bfloat16),
                   jax.ShapeDtypeStruct((b, s, IDX_DIM), jnp.bfloat16),
                   jax.ShapeDtypeStruct((b, s, IDX_HEADS), jnp.float32),
                   jax.ShapeDtypeStruct((b, N_HEADS, s, KV_RANK), jnp.bfloat16),
                   jax.ShapeDtypeStruct((b, IDX_HEADS, s, IDX_DIM), jnp.bfloat16)],
        compiler_params=_cparams("parallel", "parallel"),
        name="dsa_latents",
    )(small, q_norm.reshape(1, -1), kv_norm.reshape(1, -1), wq, wukt, wiq)


def _t5_bucket_np(dist):
    max_exact = N_BUCKETS // 2
    d = np.maximum(dist, 0)
    df = np.maximum(d, 1).astype(np.float32)
    large = max_exact + (np.log(df / np.float32(max_exact)) / np.float32(math.log(MAX_DISTANCE / max_exact))
                         * np.float32(N_BUCKETS - max_exact)).astype(np.int32)
    large = np.minimum(large, N_BUCKETS - 1)
    return np.where(d < max_exact, d, large).astype(np.int32)


def _bias_table_kernel(bucket_ref, rb_ref, o_ref):
    h = pl.program_id(1)
    bk = bucket_ref[0]
    far = rb_ref[N_BUCKETS - 1, h]
    acc = jnp.zeros(bk.shape, jnp.float32)
    for b in range(N_BUCKETS - 1):
        acc = jnp.where(bk == b, rb_ref[b, h] - far, acc)
    o_ref[0, 0] = acc * LOG2E


def _bias_tables(rel_bias, tq, tk):
    n_slots = 2 * tk // tq
    r = np.arange(tq)[:, None]
    c = np.arange(tk)[None, :]
    buckets = np.stack([_t5_bucket_np(sl * tq + r - c) for sl in range(n_slots)])
    return pl.pallas_call(
        _bias_table_kernel,
        grid=(n_slots, N_HEADS),
        in_specs=[pl.BlockSpec((1, tq, tk), lambda sl, h: (sl, 0, 0)),
                  pl.BlockSpec(memory_space=pltpu.SMEM)],
        out_specs=pl.BlockSpec((1, 1, tq, tk), lambda sl, h: (sl, h, 0, 0)),
        out_shape=jax.ShapeDtypeStruct((n_slots, N_HEADS, tq, tk), jnp.float32),
        compiler_params=_cparams("parallel", "parallel"),
        name="t5_bias_tables",
    )(jnp.asarray(buckets), rel_bias)


def _dsa_attn_kernel(iq_ref, iw_ref, ik_ref, ckv_ref, qa_ref, gate_ref, wuv_ref, tbl_ref,
                     o_ref,
                     key_ref, mb_ref, acc_ref, m_ref, l_ref, wb_ref,
                     *, topk):
    tq, tk = DSA_TQ, DSA_TK
    nrep = tk // LANES
    i = pl.program_id(1)
    t0 = i * tq
    kd = t0 // tk
    n_kt = kd + 1
    par = i % (tk // tq)

    iw = iw_ref[0]
    for h in range(IDX_HEADS):
        wb_ref[h] = jnp.broadcast_to(iw[:, h:h + 1], (tq, LANES))

    def score_tile(kt, carry):
        k0 = pl.multiple_of(kt * tk, tk)
        ikt = ik_ref[0, pl.ds(k0, tk), :]
        acc = jnp.zeros((tq, tk), jnp.float32)
        for h in range(IDX_HEADS):
            d = lax.dot_general(iq_ref[0, h], ikt, _NT, preferred_element_type=jnp.float32)
            acc = acc + jnp.maximum(d, 0.0) * _rep(wb_ref[h], nrep)
        bits = lax.bitcast_convert_type(acc, jnp.int32)
        key = jnp.where(bits < 0, bits ^ jnp.int32(0x7FFFFFFF), bits)
        row = t0 + lax.broadcasted_iota(jnp.int32, (tq, tk), 0)
        col = k0 + lax.broadcasted_iota(jnp.int32, (tq, tk), 1)
        key_ref[:, pl.ds(k0, tk)] = jnp.where(col <= row, key, jnp.int32(INT_MIN))
        return carry

    lax.fori_loop(0, n_kt, score_tile, 0)

    def count_ge(cand):
        cb = jnp.broadcast_to(cand, (tq, LANES))

        def body(kt, c):
            k0 = pl.multiple_of(kt * tk, tk)
            kk = key_ref[:, pl.ds(k0, tk)]
            for j in range(nrep):
                c = c + jnp.where(kk[:, j * LANES:(j + 1) * LANES] >= cb, 1.0, 0.0)
            return c

        c = lax.fori_loop(0, n_kt, body, jnp.zeros((tq, LANES), jnp.float32))
        return jnp.sum(c, axis=1, keepdims=True)

    kf = jnp.float32(topk)
    zero = jnp.zeros((tq, 1), jnp.int32)
    ans0 = jnp.where(count_ge(zero) >= kf, zero, jnp.int32(INT_MIN))

    def search(it, ans):
        cand = ans + jnp.left_shift(jnp.int32(1), jnp.int32(30) - it)
        return jnp.where(count_ge(cand) >= kf, cand, ans)

    thr = lax.fori_loop(0, 31, search, ans0)
    thr = jnp.maximum(thr, jnp.int32(INT_MIN + 1))
    thr_b = jnp.broadcast_to(thr, (tq, LANES))

    def mask_tile(kt, carry):
        k0 = pl.multiple_of(kt * tk, tk)
        kk = key_ref[:, pl.ds(k0, tk)]
        mb_ref[:, pl.ds(k0, tk)] = jnp.where(kk >= _rep(thr_b, nrep), 0.0, MASKED)
        return carry

    lax.fori_loop(0, n_kt, mask_tile, 0)

    m_ref[...] = jnp.full(m_ref.shape, MASKED, jnp.float32)
    l_ref[...] = jnp.zeros(l_ref.shape, jnp.float32)
    acc_ref[...] = jnp.zeros(acc_ref.shape, jnp.float32)

    def attn_tile(kt, slot):
        k0 = pl.multiple_of(kt * tk, tk)
        ck = ckv_ref[0, pl.ds(k0, tk), :]
        mbt = mb_ref[:, pl.ds(k0, tk)]
        for g in range(N_HEADS // DSA_HG):
            grows = slice(g * DSA_HG * tq, (g + 1) * DSA_HG * tq)
            qa = qa_ref[0, g * DSA_HG:(g + 1) * DSA_HG].reshape(DSA_HG * tq, KV_RANK)
            sg = lax.dot_general(qa, ck, _NT, preferred_element_type=jnp.float32)
            ps, als = [], []
            for hh in range(DSA_HG):
                h = g * DSA_HG + hh
                rows = slice(h * tq, (h + 1) * tq)
                s = sg[hh * tq:(hh + 1) * tq] + mbt
                if slot is not None:
                    s = s + tbl_ref[slot, h]
                m_prev = m_ref[rows, :]
                m_new = jnp.maximum(m_prev, jnp.max(s, axis=1, keepdims=True))
                alpha = jnp.exp2(m_prev - m_new)
                p = jnp.exp2(s - _rep(m_new, nrep))
                psum = p[:, :LANES]
                for j in range(1, nrep):
                    psum = psum + p[:, j * LANES:(j + 1) * LANES]
                l_ref[rows, :] = alpha * l_ref[rows, :] + psum
                m_ref[rows, :] = m_new
                ps.append(p.astype(jnp.bfloat16))
                als.append(_rep(alpha, KV_RANK // LANES))
            pv = jnp.dot(jnp.concatenate(ps, axis=0), ck, preferred_element_type=jnp.float32)
            acc_ref[grows, :] = acc_ref[grows, :] * jnp.concatenate(als, axis=0) + pv

    def far_tile(kt, carry):
        attn_tile(kt, None)
        return carry

    lax.fori_loop(0, jnp.maximum(kd - 1, 0), far_tile, 0)

    @pl.when(kd >= 1)
    def _():
        attn_tile(kd - 1, par + tk // tq)

    attn_tile(kd, par)

    for h in range(N_HEADS):
        rows = slice(h * tq, (h + 1) * tq)
        o_lat = (acc_ref[rows, :] / jnp.sum(l_ref[rows, :], axis=1, keepdims=True)).astype(jnp.bfloat16)
        o = jnp.dot(o_lat, wuv_ref[h], preferred_element_type=jnp.float32)
        g = gate_ref[0, :, h * HEAD_DIM:(h + 1) * HEAD_DIM].astype(jnp.float32)
        o_ref[0, :, h * HEAD_DIM:(h + 1) * HEAD_DIM] = (o * (g * jax.nn.sigmoid(g))).astype(o_ref.dtype)


def _dsa_attention(iq, iw, ik, ckv, qa, gate, wuv, tbl, topk):
    b, s, _ = ckv.shape
    tq, tk = DSA_TQ, DSA_TK
    d_inner = N_HEADS * HEAD_DIM
    const = lambda *shape: pl.BlockSpec(shape, lambda bi, i: (0,) * len(shape))
    return pl.pallas_call(
        functools.partial(_dsa_attn_kernel, topk=topk),
        grid=(b, s // tq),
        in_specs=[pl.BlockSpec((1, IDX_HEADS, tq, IDX_DIM), lambda bi, i: (bi, 0, i, 0)),
                  pl.BlockSpec((1, tq, IDX_HEADS), lambda bi, i: (bi, i, 0)),
                  pl.BlockSpec((1, s, IDX_DIM), lambda bi, i: (bi, 0, 0)),
                  pl.BlockSpec((1, s, KV_RANK), lambda bi, i: (bi, 0, 0)),
                  pl.BlockSpec((1, N_HEADS, tq, KV_RANK), lambda bi, i: (bi, 0, i, 0)),
                  pl.BlockSpec((1, tq, d_inner), lambda bi, i: (bi, i, 0)),
                  const(N_HEADS, KV_RANK, HEAD_DIM),
                  const(*tbl.shape)],
        out_specs=pl.BlockSpec((1, tq, d_inner), lambda bi, i: (bi, i, 0)),
        out_shape=jax.ShapeDtypeStruct((b, s, d_inner), jnp.bfloat16),
        scratch_shapes=[pltpu.VMEM((tq, s), jnp.int32),
                        pltpu.VMEM((tq, s), jnp.float32),
                        pltpu.VMEM((N_HEADS * tq, KV_RANK), jnp.float32),
                        pltpu.VMEM((N_HEADS * tq, LANES), jnp.float32),
                        pltpu.VMEM((N_HEADS * tq, LANES), jnp.float32),
                        pltpu.VMEM((IDX_HEADS, tq, LANES), jnp.float32)],
        compiler_params=_cparams("parallel", "arbitrary"),
        name="dsa_attention",
    )(iq, iw, ik, ckv, qa, gate, wuv, tbl)


def _fox_decay_kernel(f_ref, fb_ref, o_ref, *, chunk):
    x = f_ref[0] + fb_ref[...]
    lf = jnp.minimum(x, 0.0) - jnp.log(1.0 + jnp.exp(-jnp.abs(x)))
    s = x.shape[1]
    r = lax.broadcasted_iota(jnp.int32, (chunk, chunk), 0)
    c = lax.broadcasted_iota(jnp.int32, (chunk, chunk), 1)
    tri = jnp.where(r <= c, 1.0, 0.0).astype(jnp.bfloat16)
    carry = jnp.zeros((x.shape[0], 1), jnp.float32)
    for j in range(s // chunk):
        blk = lf[:, j * chunk:(j + 1) * chunk]
        hi = blk.astype(jnp.bfloat16)
        r1 = blk - hi.astype(jnp.float32)
        mid = r1.astype(jnp.bfloat16)
        lo = (r1 - mid.astype(jnp.float32)).astype(jnp.bfloat16)
        cs = (jnp.dot(hi, tri, preferred_element_type=jnp.float32)
              + jnp.dot(mid, tri, preferred_element_type=jnp.float32)
              + jnp.dot(lo, tri, preferred_element_type=jnp.float32)) + carry
        o_ref[0, :, j * chunk:(j + 1) * chunk] = cs * (-LOG2E)
        carry = cs[:, chunk - 1:chunk]


def _fox_decay(f_t, f_bias, chunk=512):
    b, nh, s = f_t.shape
    chunk = min(chunk, s)
    return pl.pallas_call(
        functools.partial(_fox_decay_kernel, chunk=chunk),
        grid=(b,),
        in_specs=[pl.BlockSpec((1, nh, s), lambda bi: (bi, 0, 0)),
                  pl.BlockSpec((nh, 1), lambda bi: (0, 0))],
        out_specs=pl.BlockSpec((1, nh, s), lambda bi: (bi, 0, 0)),
        out_shape=jax.ShapeDtypeStruct((b, nh, s), jnp.float32),
        compiler_params=_cparams("parallel"),
        name="fox_decay",
    )(f_t, f_bias.reshape(nh, 1))


def _fox_attn_kernel(q_ref, k_ref, v_ref, dk_ref, gate_ref, o_ref, m_ref, l_ref, acc_ref, *, tq, tk):
    nrep = tk // LANES
    i = pl.program_id(2)
    m_ref[...] = jnp.full(m_ref.shape, MASKED, jnp.float32)
    l_ref[...] = jnp.zeros(l_ref.shape, jnp.float32)
    acc_ref[...] = jnp.zeros(acc_ref.shape, jnp.float32)

    def tile(kt, masked):
        k0 = pl.multiple_of(kt * tk, tk)
        for hh in range(FOX_HG):
            cols = slice(hh * HEAD_DIM, (hh + 1) * HEAD_DIM)
            q = q_ref[0, :, cols]
            k = k_ref[0, pl.ds(k0, tk), cols]
            v = v_ref[0, pl.ds(k0, tk), cols]
            s = lax.dot_general(q, k, _NT, preferred_element_type=jnp.float32)
            s = s + dk_ref[0, hh, :, pl.ds(k0, tk)]
            if masked:
                row = lax.broadcasted_iota(jnp.int32, (tq, tk), 0)
                col = lax.broadcasted_iota(jnp.int32, (tq, tk), 1)
                s = jnp.where(col <= row, s, MASKED)
            m_prev = m_ref[hh]
            m_new = jnp.maximum(m_prev, jnp.max(s, axis=1, keepdims=True))
            alpha = jnp.exp2(m_prev - m_new)
            p = jnp.exp2(s - _rep(m_new, nrep))
            psum = p[:, :LANES]
            for j in range(1, nrep):
                psum = psum + p[:, j * LANES:(j + 1) * LANES]
            l_ref[hh] = alpha * l_ref[hh] + psum
            m_ref[hh] = m_new
            pv = jnp.dot(p.astype(jnp.bfloat16), v, preferred_element_type=jnp.float32)
            acc_ref[hh] = acc_ref[hh] * alpha + pv

    def full_tile(kt, carry):
        tile(kt, False)
        return carry

    lax.fori_loop(0, i, full_tile, 0)
    tile(i, True)

    for hh in range(FOX_HG):
        cols = slice(hh * HEAD_DIM, (hh + 1) * HEAD_DIM)
        g = gate_ref[0, :, cols].astype(jnp.float32)
        o = acc_ref[hh] / jnp.sum(l_ref[hh], axis=1, keepdims=True)
        o_ref[0, :, cols] = (o * (g * jax.nn.sigmoid(g))).astype(o_ref.dtype)


def _fox_attention(proj, dk, d_inner):
    b, s, _ = proj.shape
    tq = tk = min(FOX_TQ, s)
    nh = d_inner // HEAD_DIM
    ng = nh // FOX_HG
    w = FOX_HG * HEAD_DIM
    return pl.pallas_call(
        functools.partial(_fox_attn_kernel, tq=tq, tk=tk),
        grid=(b, ng, s // tq),
        in_specs=[pl.BlockSpec((1, tq, w), lambda bi, h, i: (bi, i, h)),
                  pl.BlockSpec((1, s, w), lambda bi, h, i: (bi, 0, ng + h)),
                  pl.BlockSpec((1, s, w), lambda bi, h, i: (bi, 0, 2 * ng + h)),
                  pl.BlockSpec((1, FOX_HG, 1, s), lambda bi, h, i: (bi, h, 0, 0)),
                  pl.BlockSpec((1, tq, w), lambda bi, h, i: (bi, i, 3 * ng + h))],
        out_specs=pl.BlockSpec((1, tq, w), lambda bi, h, i: (bi, i, h)),
        out_shape=jax.ShapeDtypeStruct((b, s, d_inner), jnp.bfloat16),
        scratch_shapes=[pltpu.VMEM((FOX_HG, tq, LANES), jnp.float32),
                        pltpu.VMEM((FOX_HG, tq, LANES), jnp.float32),
                        pltpu.VMEM((FOX_HG, tq, HEAD_DIM), jnp.float32)],
        compiler_params=_cparams("parallel", "parallel", "arbitrary"),
        name="fox_attention",
    )(proj, proj, proj, dk, proj)


def _dsa_layer(hn, h, g_next, hn_dtype, b, s, w_in, q_norm, kv_norm, w_q_up, w_uk, w_uv, w_iq, w_out, tbl):
    bf = jnp.bfloat16
    d_inner = N_HEADS * HEAD_DIM
    n_small = Q_RANK + KV_RANK + IDX_DIM + IDX_HEADS
    n_pad = -(-n_small // LANES) * LANES
    w_small = jnp.pad(w_in[:, :n_small], ((0, 0), (0, n_pad - n_small))).astype(bf)
    w_gate = w_in[:, n_small:].astype(bf)
    small = _matmul(hn, w_small, jnp.ones((n_pad,), jnp.float32), jnp.float32, 1024, n_pad)
    gate = _matmul(hn, w_gate, jnp.ones((d_inner,), jnp.float32), bf, 1024, 1024)
    wukt = jnp.transpose(w_uk, (1, 2, 0)).astype(bf)
    wuv = jnp.transpose(w_uv, (1, 0, 2)).astype(bf)
    ckv, ik, iw, qa, iq = _dsa_latents(small.reshape(b, s, n_pad), q_norm, kv_norm,
                                       w_q_up.astype(bf), wukt, w_iq.astype(bf),
                                       tm=min(512, s))
    topk = min(TOPK_MAX, s // 4)
    og = _dsa_attention(iq, iw, ik, ckv, qa, gate.reshape(b, s, d_inner), wuv, tbl, topk)
    return _outproj(og.reshape(b * s, d_inner), w_out.astype(bf), h, g_next, hn_dtype)


def _fox_layer(hn, h, g_next, hn_dtype, b, s, w_in, f_bias, w_out):
    bf = jnp.bfloat16
    d_inner = N_HEADS * HEAD_DIM
    w_qkvg = jnp.concatenate([w_in[:, :3 * d_inner], w_in[:, 3 * d_inner + N_HEADS:]], axis=1).astype(bf)
    w_f = jnp.pad(w_in[:, 3 * d_inner:3 * d_inner + N_HEADS], ((0, 0), (0, LANES - N_HEADS))).astype(bf)
    colscale = jnp.concatenate([jnp.full((d_inner,), HEAD_DIM ** -0.5 * LOG2E, jnp.float32),
                                jnp.ones((3 * d_inner,), jnp.float32)])
    proj = _matmul(hn, w_qkvg, colscale, bf, 1024, 1024)
    f_pre = _matmul(hn, w_f, jnp.ones((LANES,), jnp.float32), jnp.float32, 1024, LANES)
    f_t = jnp.transpose(f_pre.reshape(b, s, LANES)[:, :, :N_HEADS], (0, 2, 1))
    dk = _fox_decay(f_t, f_bias).reshape(b, N_HEADS, 1, s)
    og = _fox_attention(proj.reshape(b, s, 4 * d_inner), dk, d_inner)
    return _outproj(og.reshape(b * s, d_inner), w_out.astype(bf), h, g_next, hn_dtype)


def kernel(x, norm_g, final_g, rel_bias, a_w_in, a_q_norm, a_kv_norm, a_w_q_up, a_w_uk, a_w_uv,
           a_w_iq, a_w_out, b_w_in, b_f_bias, b_w_out):
    b, s, d = x.shape
    depth = norm_g.shape[0]
    tbl = _bias_tables(rel_bias, DSA_TQ, DSA_TK)
    h = x.reshape(b * s, d)
    hn = _rmsnorm(h, norm_g[0], jnp.bfloat16)
    for i in range(depth):
        last = i == depth - 1
        g_next = final_g if last else norm_g[i + 1]
        hn_dtype = x.dtype if last else jnp.bfloat16
        j = i // 2
        if i % 2 == 0:
            h, hn = _dsa_layer(hn, h, g_next, hn_dtype, b, s, a_w_in[j], a_q_norm[j], a_kv_norm[j],
                               a_w_q_up[j], a_w_uk[j], a_w_uv[j], a_w_iq[j], a_w_out[j], tbl)
        else:
            h, hn = _fox_layer(hn, h, g_next, hn_dtype, b, s, b_w_in[j], b_f_bias[j], b_w_out[j])
    return hn.reshape(b, s, d)
```

```python
import functools
import math

import numpy as np
import jax
import jax.numpy as jnp
from jax import lax
from jax.experimental import pallas as pl
from jax.experimental.pallas import tpu as pltpu

N_HEADS = 16
HEAD_DIM = 128
Q_RANK = 512
KV_RANK = 256
IDX_HEADS = 16
IDX_DIM = 64
TOPK_MAX = 256
N_BUCKETS = 32
MAX_DISTANCE = 128
EPS = 1e-6

LOG2E = 1.4426950408889634
MASKED = -1e30
INT_MIN = -2147483648
LANES = 128
SUBLANES = 8
BF16_ROWS = 16
VMEM_LIMIT = 48 * 1024 * 1024

DSA_TQ = 128
DSA_TK = 256
DSA_HG = 16
FOX_TQ = 512
FOX_TK = 512
FOX_HG = 4
FOX_DK_PIECES = 3

_NT = (((1,), (1,)), ((), ()))


def _cparams(*sem):
    return pltpu.CompilerParams(dimension_semantics=sem, vmem_limit_bytes=VMEM_LIMIT)


def _colmax(s):
    parts = [s[j * SUBLANES:(j + 1) * SUBLANES, :] for j in range(s.shape[0] // SUBLANES)]
    while len(parts) > 1:
        parts = [jnp.maximum(parts[j], parts[j + 1]) for j in range(0, len(parts), 2)]
    return jnp.max(parts[0], axis=0, keepdims=True)


def _split_bf16(x, n):
    pieces = []
    for _ in range(n):
        p = x.astype(jnp.bfloat16)
        pieces.append(p)
        x = x - p.astype(jnp.float32)
    return pieces


def _rmsnorm_kernel(x_ref, g_ref, o_ref):
    x = x_ref[...]
    ms = jnp.mean(x * x, axis=-1, keepdims=True)
    o_ref[...] = (x * lax.rsqrt(ms + EPS) * g_ref[...]).astype(o_ref.dtype)


def _rmsnorm(x2d, g, out_dtype, tm=512):
    m, d = x2d.shape
    return pl.pallas_call(
        _rmsnorm_kernel,
        grid=(m // tm,),
        in_specs=[pl.BlockSpec((tm, d), lambda i: (i, 0)),
                  pl.BlockSpec((1, d), lambda i: (0, 0))],
        out_specs=pl.BlockSpec((tm, d), lambda i: (i, 0)),
        out_shape=jax.ShapeDtypeStruct((m, d), out_dtype),
        compiler_params=_cparams("parallel"),
        name="rmsnorm",
    )(x2d, g.reshape(1, d))


def _matmul_kernel(x_ref, w_ref, cs_ref, o_ref):
    acc = jnp.dot(x_ref[...], w_ref[...], preferred_element_type=jnp.float32)
    o_ref[...] = (acc * cs_ref[...]).astype(o_ref.dtype)


def _matmul(x, w, colscale, out_dtype, tm, tn):
    m, k = x.shape
    n = w.shape[1]
    tm, tn = min(tm, m), min(tn, n)
    return pl.pallas_call(
        _matmul_kernel,
        grid=(n // tn, m // tm),
        in_specs=[pl.BlockSpec((tm, k), lambda j, i: (i, 0)),
                  pl.BlockSpec((k, tn), lambda j, i: (0, j)),
                  pl.BlockSpec((1, tn), lambda j, i: (0, j))],
        out_specs=pl.BlockSpec((tm, tn), lambda j, i: (i, j)),
        out_shape=jax.ShapeDtypeStruct((m, n), out_dtype),
        compiler_params=_cparams("parallel", "parallel"),
        name="matmul",
    )(x, w, colscale.reshape(1, n))


def _matmul_t_kernel(wt_ref, x_ref, o_ref):
    o_ref[0] = lax.dot_general(wt_ref[...], x_ref[0], _NT,
                               preferred_element_type=jnp.float32).astype(o_ref.dtype)


def _matmul_t(x, wt, out_dtype, tm, tn):
    b, s, k = x.shape
    n = wt.shape[0]
    tm, tn = min(tm, s), min(tn, n)
    return pl.pallas_call(
        _matmul_t_kernel,
        grid=(n // tn, b, s // tm),
        in_specs=[pl.BlockSpec((tn, k), lambda j, bi, i: (j, 0)),
                  pl.BlockSpec((1, tm, k), lambda j, bi, i: (bi, i, 0))],
        out_specs=pl.BlockSpec((1, tn, tm), lambda j, bi, i: (bi, j, i)),
        out_shape=jax.ShapeDtypeStruct((b, n, s), out_dtype),
        compiler_params=_cparams("parallel", "parallel", "parallel"),
        name="matmul_t",
    )(wt, x)


def _outproj_kernel(x_ref, w_ref, h_ref, g_ref, hnew_ref, hn_ref):
    y = jnp.dot(x_ref[...], w_ref[...], preferred_element_type=jnp.float32)
    hnew = h_ref[...] + y
    hnew_ref[...] = hnew
    ms = jnp.mean(hnew * hnew, axis=-1, keepdims=True)
    hn_ref[...] = (hnew * lax.rsqrt(ms + EPS) * g_ref[...]).astype(hn_ref.dtype)


def _outproj(x, w, h, g_next, hn_dtype, tm=256):
    m, k = x.shape
    d = w.shape[1]
    return pl.pallas_call(
        _outproj_kernel,
        grid=(m // tm,),
        in_specs=[pl.BlockSpec((tm, k), lambda i: (i, 0)),
                  pl.BlockSpec((k, d), lambda i: (0, 0)),
                  pl.BlockSpec((tm, d), lambda i: (i, 0)),
                  pl.BlockSpec((1, d), lambda i: (0, 0))],
        out_specs=[pl.BlockSpec((tm, d), lambda i: (i, 0)),
                   pl.BlockSpec((tm, d), lambda i: (i, 0))],
        out_shape=[jax.ShapeDtypeStruct((m, d), jnp.float32),
                   jax.ShapeDtypeStruct((m, d), hn_dtype)],
        compiler_params=_cparams("parallel"),
        name="outproj",
    )(x, w, h, g_next.reshape(1, d))


def _dsa_latent_kernel(sm_ref, qn_ref, kvn_ref, wq_ref, wukt_ref, wiq_ref,
                       ckv_ref, ckvt_ref, ik_ref, iw_ref, qa_ref, iq_ref):
    sm = sm_ref[0]
    o1, o2, o3, o4 = Q_RANK, Q_RANK + KV_RANK, Q_RANK + KV_RANK + IDX_DIM, \
        Q_RANK + KV_RANK + IDX_DIM + IDX_HEADS

    def rms(v, g):
        ms = jnp.mean(v * v, axis=-1, keepdims=True)
        return v * lax.rsqrt(ms + EPS) * g

    cq = rms(sm[:, :o1], qn_ref[...]).astype(jnp.bfloat16)
    ckv = rms(sm[:, o1:o2], kvn_ref[...])
    ckv_ref[0] = ckv.astype(jnp.bfloat16)
    ckvt_ref[0, :KV_RANK, :] = ckv.T.astype(jnp.bfloat16)
    ckvt_ref[0, KV_RANK:, :] = jnp.ones((BF16_ROWS, sm.shape[0]), jnp.bfloat16)
    ik_ref[0] = sm[:, o2:o3].astype(jnp.bfloat16)
    iw_ref[0] = sm[:, o3:o4] * (IDX_HEADS ** -0.5 * IDX_DIM ** -0.5)
    q = jnp.dot(cq, wq_ref[...], preferred_element_type=jnp.float32).astype(jnp.bfloat16)
    iq = jnp.dot(cq, wiq_ref[...], preferred_element_type=jnp.float32).astype(jnp.bfloat16)
    for h in range(N_HEADS):
        qh = q[:, h * HEAD_DIM:(h + 1) * HEAD_DIM]
        qa = jnp.dot(qh, wukt_ref[h], preferred_element_type=jnp.float32)
        qa_ref[0, h] = (qa * (HEAD_DIM ** -0.5 * LOG2E)).astype(jnp.bfloat16)
    for h in range(IDX_HEADS):
        iq_ref[0, h] = iq[:, h * IDX_DIM:(h + 1) * IDX_DIM]


def _dsa_latents(small, q_norm, kv_norm, wq, wukt, wiq, tm=512):
    b, s, ns = small.shape
    full = lambda *shape: pl.BlockSpec(shape, lambda bi, i: (0,) * len(shape))
    vrows = KV_RANK + BF16_ROWS
    return pl.pallas_call(
        _dsa_latent_kernel,
        grid=(b, s // tm),
        in_specs=[pl.BlockSpec((1, tm, ns), lambda bi, i: (bi, i, 0)),
                  full(1, Q_RANK), full(1, KV_RANK),
                  full(Q_RANK, N_HEADS * HEAD_DIM),
                  full(N_HEADS, HEAD_DIM, KV_RANK),
                  full(Q_RANK, IDX_HEADS * IDX_DIM)],
        out_specs=[pl.BlockSpec((1, tm, KV_RANK), lambda bi, i: (bi, i, 0)),
                   pl.BlockSpec((1, vrows, tm), lambda bi, i: (bi, 0, i)),
                   pl.BlockSpec((1, tm, IDX_DIM), lambda bi, i: (bi, i, 0)),
                   pl.BlockSpec((1, tm, IDX_HEADS), lambda bi, i: (bi, i, 0)),
                   pl.BlockSpec((1, N_HEADS, tm, KV_RANK), lambda bi, i: (bi, 0, i, 0)),
                   pl.BlockSpec((1, IDX_HEADS, tm, IDX_DIM), lambda bi, i: (bi, 0, i, 0))],
        out_shape=[jax.ShapeDtypeStruct((b, s, KV_RANK), jnp.bfloat16),
                   jax.ShapeDtypeStruct((b, vrows, s), jnp.bfloat16),
                   jax.ShapeDtypeStruct((b, s, IDX_DIM), jnp.bfloat16),
                   jax.ShapeDtypeStruct((b, s, IDX_HEADS), jnp.float32),
                   jax.ShapeDtypeStruct((b, N_HEADS, s, KV_RANK), jnp.bfloat16),
                   jax.ShapeDtypeStruct((b, IDX_HEADS, s, IDX_DIM), jnp.bfloat16)],
        compiler_params=_cparams("parallel", "parallel"),
        name="dsa_latents",
    )(small, q_norm.reshape(1, -1), kv_norm.reshape(1, -1), wq, wukt, wiq)


def _t5_bucket_np(dist):
    max_exact = N_BUCKETS // 2
    d = np.maximum(dist, 0)
    df = np.maximum(d, 1).astype(np.float32)
    large = max_exact + (np.log(df / np.float32(max_exact)) / np.float32(math.log(MAX_DISTANCE / max_exact))
                         * np.float32(N_BUCKETS - max_exact)).astype(np.int32)
    large = np.minimum(large, N_BUCKETS - 1)
    return np.where(d < max_exact, d, large).astype(np.int32)


def _bias_table_kernel(bucket_ref, rb_ref, o_ref):
    h = pl.program_id(1)
    bk = bucket_ref[0]
    far = rb_ref[N_BUCKETS - 1, h]
    acc = jnp.zeros(bk.shape, jnp.float32)
    for b in range(N_BUCKETS - 1):
        acc = jnp.where(bk == b, rb_ref[b, h] - far, acc)
    o_ref[0, 0] = acc * LOG2E


def _bias_tables(rel_bias, tq, tk):
    n_slots = 2 * tk // tq
    j = np.arange(tk)[:, None]
    r = np.arange(tq)[None, :]
    buckets = np.stack([_t5_bucket_np(sl * tq + r - j) for sl in range(n_slots)])
    return pl.pallas_call(
        _bias_table_kernel,
        grid=(n_slots, N_HEADS),
        in_specs=[pl.BlockSpec((1, tk, tq), lambda sl, h: (sl, 0, 0)),
                  pl.BlockSpec(memory_space=pltpu.SMEM)],
        out_specs=pl.BlockSpec((1, 1, tk, tq), lambda sl, h: (sl, h, 0, 0)),
        out_shape=jax.ShapeDtypeStruct((n_slots, N_HEADS, tk, tq), jnp.float32),
        compiler_params=_cparams("parallel", "parallel"),
        name="t5_bias_tables",
    )(jnp.asarray(buckets), rel_bias)


def _dsa_attn_kernel(iq_ref, iwt_ref, ik_ref, ckv_ref, ckvt_ref, qa_ref, gate_ref, wuvt_ref, tbl_ref,
                     o_ref,
                     key_ref, acc_ref, m_ref,
                     *, topk):
    tq, tk = DSA_TQ, DSA_TK
    i = pl.program_id(1)
    t0 = i * tq
    kd = t0 // tk
    n_kt = kd + 1
    par = i % (tk // tq)

    sw = 2 * tk
    iq_all = iq_ref[0].reshape(IDX_HEADS * tq, IDX_DIM)

    def score_tile(kt, carry):
        k0 = pl.multiple_of(kt * sw, sw)
        d = lax.dot_general(ik_ref[0, pl.ds(k0, sw), :], iq_all, _NT,
                            preferred_element_type=jnp.float32)
        acc = jnp.zeros((sw, tq), jnp.float32)
        for h in range(IDX_HEADS):
            acc = acc + jnp.maximum(d[:, h * tq:(h + 1) * tq], 0.0) * iwt_ref[0, h:h + 1, :]
        bits = lax.bitcast_convert_type(acc, jnp.int32)
        key = jnp.where(bits < 0, bits ^ jnp.int32(0x7FFFFFFF), bits)
        srow = k0 + lax.broadcasted_iota(jnp.int32, (sw, tq), 0)
        tcol = t0 + lax.broadcasted_iota(jnp.int32, (sw, tq), 1)
        key_ref[pl.ds(k0, sw), :] = jnp.where(srow <= tcol, key, jnp.int32(INT_MIN))
        return carry

    lax.fori_loop(0, (n_kt + 1) // 2, score_tile, 0)

    def count_ge(cand):
        def body(kt, c):
            k0 = pl.multiple_of(kt * tk, tk)
            ind = jnp.where(key_ref[pl.ds(k0, tk), :] >= cand, 1.0, 0.0)
            parts = [ind[j * SUBLANES:(j + 1) * SUBLANES, :] for j in range(tk // SUBLANES)]
            while len(parts) > 1:
                parts = [parts[j] + parts[j + 1] for j in range(0, len(parts), 2)]
            return c + parts[0]

        c = lax.fori_loop(0, n_kt, body, jnp.zeros((SUBLANES, tq), jnp.float32))
        return jnp.sum(c, axis=0, keepdims=True)

    kf = jnp.float32(topk)
    zero = jnp.zeros((1, tq), jnp.int32)
    ans0 = jnp.where(count_ge(zero) >= kf, zero, jnp.int32(INT_MIN))

    def search(it, ans):
        cand = ans + jnp.left_shift(jnp.int32(1), jnp.int32(30) - it)
        return jnp.where(count_ge(cand) >= kf, cand, ans)

    thr = lax.fori_loop(0, 31, search, ans0)
    thr = jnp.maximum(thr, jnp.int32(INT_MIN + 1))

    def mask_tile(kt, carry):
        k0 = pl.multiple_of(kt * tk, tk)
        mb = jnp.where(key_ref[pl.ds(k0, tk), :] >= thr, 0.0, MASKED)
        key_ref[pl.ds(k0, tk), :] = lax.bitcast_convert_type(mb, jnp.int32)
        return carry

    lax.fori_loop(0, n_kt, mask_tile, 0)

    m_ref[...] = jnp.full(m_ref.shape, MASKED, jnp.float32)
    acc_ref[...] = jnp.zeros(acc_ref.shape, jnp.float32)

    qa = qa_ref[0].reshape(N_HEADS * tq, KV_RANK)

    def attn_tile(k0, width, slot):
        ck = ckv_ref[0, pl.ds(k0, width), :]
        ckt = ckvt_ref[0, :, pl.ds(k0, width)]
        mbt = lax.bitcast_convert_type(key_ref[pl.ds(k0, width), :], jnp.float32)
        st = lax.dot_general(ck, qa, _NT, preferred_element_type=jnp.float32)
        ps, als = [], []
        for h in range(N_HEADS):
            cols = slice(h * tq, (h + 1) * tq)
            s = st[:, cols] + mbt
            if slot is not None:
                s = s + tbl_ref[slot, h]
            m_prev = m_ref[:, cols]
            m_new = jnp.maximum(m_prev, _colmax(s))
            m_ref[:, cols] = m_new
            ps.append(jnp.exp2(s - m_new).astype(jnp.bfloat16))
            als.append(jnp.exp2(m_prev - m_new))
        pv = jnp.dot(ckt, jnp.concatenate(ps, axis=1), preferred_element_type=jnp.float32)
        acc_ref[...] = acc_ref[...] * jnp.concatenate(als, axis=1) + pv

    n_far = jnp.maximum(kd - 1, 0)
    n_wide = n_far // 2

    def wide_tile(p, carry):
        attn_tile(pl.multiple_of(p * 2 * tk, 2 * tk), 2 * tk, None)
        return carry

    lax.fori_loop(0, n_wide, wide_tile, 0)

    @pl.when(n_far % 2 == 1)
    def _():
        attn_tile(pl.multiple_of(n_wide * 2 * tk, tk), tk, None)

    @pl.when(kd >= 1)
    def _():
        attn_tile(pl.multiple_of((kd - 1) * tk, tk), tk, par + tk // tq)

    attn_tile(pl.multiple_of(kd * tk, tk), tk, par)

    for h in range(N_HEADS):
        a = acc_ref[:, h * tq:(h + 1) * tq]
        o_lat_t = (a[:KV_RANK] / a[KV_RANK:KV_RANK + 1]).astype(jnp.bfloat16)
        o = jnp.dot(wuvt_ref[h], o_lat_t, preferred_element_type=jnp.float32).T
        g = gate_ref[0, :, h * HEAD_DIM:(h + 1) * HEAD_DIM].astype(jnp.float32)
        o_ref[0, :, h * HEAD_DIM:(h + 1) * HEAD_DIM] = (o * (g * jax.nn.sigmoid(g))).astype(o_ref.dtype)


def _dsa_attention(iq, iwt, ik, ckv, ckvt, qa, gate, wuvt, tbl, topk):
    b, s, _ = ckv.shape
    tq, tk = DSA_TQ, DSA_TK
    d_inner = N_HEADS * HEAD_DIM
    vrows = ckvt.shape[1]
    const = lambda *shape: pl.BlockSpec(shape, lambda bi, i: (0,) * len(shape))
    return pl.pallas_call(
        functools.partial(_dsa_attn_kernel, topk=topk),
        grid=(b, s // tq),
        in_specs=[pl.BlockSpec((1, IDX_HEADS, tq, IDX_DIM), lambda bi, i: (bi, 0, i, 0)),
                  pl.BlockSpec((1, IDX_HEADS, tq), lambda bi, i: (bi, 0, i)),
                  pl.BlockSpec((1, s, IDX_DIM), lambda bi, i: (bi, 0, 0)),
                  pl.BlockSpec((1, s, KV_RANK), lambda bi, i: (bi, 0, 0)),
                  pl.BlockSpec((1, vrows, s), lambda bi, i: (bi, 0, 0)),
                  pl.BlockSpec((1, N_HEADS, tq, KV_RANK), lambda bi, i: (bi, 0, i, 0)),
                  pl.BlockSpec((1, tq, d_inner), lambda bi, i: (bi, i, 0)),
                  const(N_HEADS, HEAD_DIM, KV_RANK),
                  const(*tbl.shape)],
        out_specs=pl.BlockSpec((1, tq, d_inner), lambda bi, i: (bi, i, 0)),
        out_shape=jax.ShapeDtypeStruct((b, s, d_inner), jnp.bfloat16),
        scratch_shapes=[pltpu.VMEM((s, tq), jnp.int32),
                        pltpu.VMEM((vrows, N_HEADS * tq), jnp.float32),
                        pltpu.VMEM((1, N_HEADS * tq), jnp.float32)],
        compiler_params=_cparams("parallel", "arbitrary"),
        name="dsa_attention",
    )(iq, iwt, ik, ckv, ckvt, qa, gate, wuvt, tbl)


def _fox_decay_kernel(f_ref, fb_ref, place_ref, o_ref, carry_ref):
    j = pl.program_id(1)

    @pl.when(j == 0)
    def _():
        carry_ref[...] = jnp.zeros(carry_ref.shape, jnp.float32)

    x = f_ref[0][:, :N_HEADS] + fb_ref[...]
    lf = jnp.minimum(x, 0.0) - jnp.log(1.0 + jnp.exp(-jnp.abs(x)))
    chunk = x.shape[0]
    r = lax.broadcasted_iota(jnp.int32, (chunk, chunk), 0)
    c = lax.broadcasted_iota(jnp.int32, (chunk, chunk), 1)
    tri = jnp.where(c <= r, 1.0, 0.0).astype(jnp.bfloat16)
    cs = carry_ref[0:1, :N_HEADS]
    for piece in _split_bf16(lf, 3):
        cs = cs + jnp.dot(tri, piece, preferred_element_type=jnp.float32)
    carry_ref[0:1, :N_HEADS] = cs[chunk - 1:chunk, :]
    out = jnp.zeros(o_ref.shape[1:], jnp.float32)
    for ci, piece in enumerate(_split_bf16(cs * (-LOG2E), FOX_DK_PIECES)):
        out = out + jnp.dot(piece, place_ref[ci], preferred_element_type=jnp.float32)
    o_ref[0] = out.astype(o_ref.dtype)


def _fox_decay(f_pre, f_bias, chunk=512):
    b, s, fw = f_pre.shape
    chunk = min(chunk, s)
    width = N_HEADS * HEAD_DIM
    place = np.zeros((FOX_DK_PIECES, N_HEADS, width), np.float32)
    for ci in range(FOX_DK_PIECES):
        place[ci, np.arange(N_HEADS), np.arange(N_HEADS) * HEAD_DIM + ci] = 1.0
    return pl.pallas_call(
        _fox_decay_kernel,
        grid=(b, s // chunk),
        in_specs=[pl.BlockSpec((1, chunk, fw), lambda bi, j: (bi, j, 0)),
                  pl.BlockSpec((1, N_HEADS), lambda bi, j: (0, 0)),
                  pl.BlockSpec((FOX_DK_PIECES, N_HEADS, width), lambda bi, j: (0, 0, 0))],
        out_specs=pl.BlockSpec((1, chunk, width), lambda bi, j: (bi, j, 0)),
        out_shape=jax.ShapeDtypeStruct((b, s, width), jnp.bfloat16),
        scratch_shapes=[pltpu.VMEM((SUBLANES, LANES), jnp.float32)],
        compiler_params=_cparams("parallel", "arbitrary"),
        name="fox_decay",
    )(f_pre, f_bias.reshape(1, N_HEADS), jnp.asarray(place, jnp.bfloat16))


def _fox_attn_kernel(q_ref, k_ref, kx_ref, vt_ref, gate_ref, o_ref, m_ref, acc_ref, *, tq, tk):
    i = pl.program_id(2)
    m_ref[...] = jnp.full(m_ref.shape, MASKED, jnp.float32)
    acc_ref[...] = jnp.zeros(acc_ref.shape, jnp.float32)
    lane = lax.broadcasted_iota(jnp.int32, (tq, HEAD_DIM), 1)
    qx = jnp.where(lane < FOX_DK_PIECES, 1.0, 0.0).astype(jnp.bfloat16)
    ones = jnp.ones((BF16_ROWS, tk), jnp.bfloat16)

    def tile(kt, masked):
        k0 = pl.multiple_of(kt * tk, tk)

        def scores_t(hh):
            cols = slice(hh * HEAD_DIM, (hh + 1) * HEAD_DIM)
            qc = jnp.concatenate([q_ref[0, :, cols], qx], axis=1)
            kc = jnp.concatenate([k_ref[0, pl.ds(k0, tk), cols],
                                  kx_ref[0, pl.ds(k0, tk), cols]], axis=1)
            return lax.dot_general(kc, qc, _NT, preferred_element_type=jnp.float32)

        sts = [scores_t(0)]
        for hh in range(FOX_HG):
            if hh + 1 < FOX_HG:
                sts.append(scores_t(hh + 1))
            cols = slice(hh * HEAD_DIM, (hh + 1) * HEAD_DIM)
            s = sts[hh]
            if masked:
                srow = lax.broadcasted_iota(jnp.int32, (tk, tq), 0)
                tcol = lax.broadcasted_iota(jnp.int32, (tk, tq), 1)
                s = jnp.where(srow <= tcol, s, MASKED)
            m_prev = m_ref[hh]
            m_new = jnp.maximum(m_prev, _colmax(s))
            alpha = jnp.exp2(m_prev - m_new)
            p = jnp.exp2(s - m_new).astype(jnp.bfloat16)
            m_ref[hh] = m_new
            vt = jnp.concatenate([vt_ref[0, cols, pl.ds(k0, tk)], ones], axis=0)
            acc_ref[hh] = acc_ref[hh] * alpha + jnp.dot(vt, p, preferred_element_type=jnp.float32)

    def full_tile(kt, carry):
        tile(kt, False)
        return carry

    lax.fori_loop(0, i, full_tile, 0)
    tile(i, True)

    for hh in range(FOX_HG):
        cols = slice(hh * HEAD_DIM, (hh + 1) * HEAD_DIM)
        a = acc_ref[hh]
        o = (a[:HEAD_DIM] / a[HEAD_DIM:HEAD_DIM + 1]).T
        g = gate_ref[0, :, cols].astype(jnp.float32)
        o_ref[0, :, cols] = (o * (g * jax.nn.sigmoid(g))).astype(o_ref.dtype)


def _fox_attention(proj, kx, vt, d_inner):
    b, s, _ = proj.shape
    tq = tk = min(FOX_TQ, s)
    nh = d_inner // HEAD_DIM
    ng = nh // FOX_HG
    w = FOX_HG * HEAD_DIM
    return pl.pallas_call(
        functools.partial(_fox_attn_kernel, tq=tq, tk=tk),
        grid=(b, ng, s // tq),
        in_specs=[pl.BlockSpec((1, tq, w), lambda bi, h, i: (bi, i, h)),
                  pl.BlockSpec((1, s, w), lambda bi, h, i: (bi, 0, ng + h)),
                  pl.BlockSpec((1, s, w), lambda bi, h, i: (bi, 0, h)),
                  pl.BlockSpec((1, w, s), lambda bi, h, i: (bi, h, 0)),
                  pl.BlockSpec((1, tq, w), lambda bi, h, i: (bi, i, 2 * ng + h))],
        out_specs=pl.BlockSpec((1, tq, w), lambda bi, h, i: (bi, i, h)),
        out_shape=jax.ShapeDtypeStruct((b, s, d_inner), jnp.bfloat16),
        scratch_shapes=[pltpu.VMEM((FOX_HG, 1, tq), jnp.float32),
                        pltpu.VMEM((FOX_HG, HEAD_DIM + BF16_ROWS, tq), jnp.float32)],
        compiler_params=_cparams("parallel", "parallel", "arbitrary"),
        name="fox_attention",
    )(proj, proj, kx, vt, proj)


def _dsa_layer(hn, h, g_next, hn_dtype, b, s, w_in, q_norm, kv_norm, w_q_up, w_uk, w_uv, w_iq, w_out, tbl):
    bf = jnp.bfloat16
    d_inner = N_HEADS * HEAD_DIM
    n_small = Q_RANK + KV_RANK + IDX_DIM + IDX_HEADS
    n_pad = -(-n_small // LANES) * LANES
    w_small = jnp.pad(w_in[:, :n_small], ((0, 0), (0, n_pad - n_small))).astype(bf)
    w_gate = w_in[:, n_small:].astype(bf)
    small = _matmul(hn, w_small, jnp.ones((n_pad,), jnp.float32), jnp.float32, 1024, n_pad)
    gate = _matmul(hn, w_gate, jnp.ones((d_inner,), jnp.float32), bf, 1024, 1024)
    wukt = jnp.transpose(w_uk, (1, 2, 0)).astype(bf)
    wuvt = jnp.transpose(w_uv, (1, 2, 0)).astype(bf)
    ckv, ckvt, ik, iw, qa, iq = _dsa_latents(small.reshape(b, s, n_pad), q_norm, kv_norm,
                                             w_q_up.astype(bf), wukt, w_iq.astype(bf),
                                             tm=min(512, s))
    iwt = jnp.transpose(iw, (0, 2, 1))
    topk = min(TOPK_MAX, s // 4)
    og = _dsa_attention(iq, iwt, ik, ckv, ckvt, qa, gate.reshape(b, s, d_inner), wuvt, tbl, topk)
    return _outproj(og.reshape(b * s, d_inner), w_out.astype(bf), h, g_next, hn_dtype)


def _fox_layer(hn, h, g_next, hn_dtype, b, s, w_in, f_bias, w_out):
    bf = jnp.bfloat16
    d_inner = N_HEADS * HEAD_DIM
    w_qkg = jnp.concatenate([w_in[:, :2 * d_inner], w_in[:, 3 * d_inner + N_HEADS:]], axis=1).astype(bf)
    w_vt = jnp.transpose(w_in[:, 2 * d_inner:3 * d_inner]).astype(bf)
    w_f = jnp.pad(w_in[:, 3 * d_inner:3 * d_inner + N_HEADS], ((0, 0), (0, LANES - N_HEADS))).astype(bf)
    colscale = jnp.concatenate([jnp.full((d_inner,), HEAD_DIM ** -0.5 * LOG2E, jnp.float32),
                                jnp.ones((2 * d_inner,), jnp.float32)])
    proj = _matmul(hn, w_qkg, colscale, bf, 1024, 1024)
    vt = _matmul_t(hn.reshape(b, s, -1), w_vt, bf, 1024, 1024)
    f_pre = _matmul(hn, w_f, jnp.ones((LANES,), jnp.float32), jnp.float32, 1024, LANES)
    kx = _fox_decay(f_pre.reshape(b, s, LANES), f_bias)
    og = _fox_attention(proj.reshape(b, s, 3 * d_inner), kx, vt, d_inner)
    return _outproj(og.reshape(b * s, d_inner), w_out.astype(bf), h, g_next, hn_dtype)


def kernel(x, norm_g, final_g, rel_bias, a_w_in, a_q_norm, a_kv_norm, a_w_q_up, a_w_uk, a_w_uv,
           a_w_iq, a_w_out, b_w_in, b_f_bias, b_w_out):
    b, s, d = x.shape
    depth = norm_g.shape[0]
    tbl = _bias_tables(rel_bias, DSA_TQ, DSA_TK)
    h = x.reshape(b * s, d)
    hn = _rmsnorm(h, norm_g[0], jnp.bfloat16)
    for i in range(depth):
        last = i == depth - 1
        g_next = final_g if last else norm_g[i + 1]
        hn_dtype = x.dtype if last else jnp.bfloat16
        j = i // 2
        if i % 2 == 0:
            h, hn = _dsa_layer(hn, h, g_next, hn_dtype, b, s, a_w_in[j], a_q_norm[j], a_kv_norm[j],
                               a_w_q_up[j], a_w_uk[j], a_w_uv[j], a_w_iq[j], a_w_out[j], tbl)
        else:
            h, hn = _fox_layer(hn, h, g_next, hn_dtype, b, s, b_w_in[j], b_f_bias[j], b_w_out[j])
    return hn.reshape(b, s, d)
```

```python
import functools
import math

import numpy as np
import jax
import jax.numpy as jnp
from jax import lax
from jax.experimental import pallas as pl
from jax.experimental.pallas import tpu as pltpu

N_HEADS = 16
HEAD_DIM = 128
Q_RANK = 512
KV_RANK = 256
IDX_HEADS = 16
IDX_DIM = 64
TOPK_MAX = 256
N_BUCKETS = 32
MAX_DISTANCE = 128
EPS = 1e-6

LOG2E = 1.4426950408889634
MASKED = -1e30
INT_MIN = -2147483648
LANES = 128
SUBLANES = 8
BF16_ROWS = 16
VMEM_LIMIT = 48 * 1024 * 1024

DSA_TQ = 128
DSA_TK = 256
DSA_HG = 16
FOX_TQ = 512
FOX_TK = 512
FOX_HG = 4
FOX_DK_PIECES = 3

_NT = (((1,), (1,)), ((), ()))


def _cparams(*sem):
    return pltpu.CompilerParams(dimension_semantics=sem, vmem_limit_bytes=VMEM_LIMIT)


def _colmax(s):
    parts = [s[j * SUBLANES:(j + 1) * SUBLANES, :] for j in range(s.shape[0] // SUBLANES)]
    while len(parts) > 1:
        parts = [jnp.maximum(parts[j], parts[j + 1]) for j in range(0, len(parts), 2)]
    return jnp.max(parts[0], axis=0, keepdims=True)


def _split_bf16(x, n):
    pieces = []
    for _ in range(n):
        p = x.astype(jnp.bfloat16)
        pieces.append(p)
        x = x - p.astype(jnp.float32)
    return pieces


def _rmsnorm_kernel(x_ref, g_ref, o_ref):
    x = x_ref[...]
    ms = jnp.mean(x * x, axis=-1, keepdims=True)
    o_ref[...] = (x * lax.rsqrt(ms + EPS) * g_ref[...]).astype(o_ref.dtype)


def _rmsnorm(x2d, g, out_dtype, tm=512):
    m, d = x2d.shape
    return pl.pallas_call(
        _rmsnorm_kernel,
        grid=(m // tm,),
        in_specs=[pl.BlockSpec((tm, d), lambda i: (i, 0)),
                  pl.BlockSpec((1, d), lambda i: (0, 0))],
        out_specs=pl.BlockSpec((tm, d), lambda i: (i, 0)),
        out_shape=jax.ShapeDtypeStruct((m, d), out_dtype),
        compiler_params=_cparams("parallel"),
        name="rmsnorm",
    )(x2d, g.reshape(1, d))


def _matmul_kernel(x_ref, w_ref, cs_ref, o_ref):
    acc = jnp.dot(x_ref[...], w_ref[...], preferred_element_type=jnp.float32)
    o_ref[...] = (acc * cs_ref[...]).astype(o_ref.dtype)


def _matmul(x, w, colscale, out_dtype, tm, tn):
    m, k = x.shape
    n = w.shape[1]
    tm, tn = min(tm, m), min(tn, n)
    return pl.pallas_call(
        _matmul_kernel,
        grid=(n // tn, m // tm),
        in_specs=[pl.BlockSpec((tm, k), lambda j, i: (i, 0)),
                  pl.BlockSpec((k, tn), lambda j, i: (0, j)),
                  pl.BlockSpec((1, tn), lambda j, i: (0, j))],
        out_specs=pl.BlockSpec((tm, tn), lambda j, i: (i, j)),
        out_shape=jax.ShapeDtypeStruct((m, n), out_dtype),
        compiler_params=_cparams("parallel", "parallel"),
        name="matmul",
    )(x, w, colscale.reshape(1, n))


def _matmul_t_kernel(wt_ref, x_ref, o_ref):
    o_ref[0] = lax.dot_general(wt_ref[...], x_ref[0], _NT,
                               preferred_element_type=jnp.float32).astype(o_ref.dtype)


def _matmul_t(x, wt, out_dtype, tm, tn):
    b, s, k = x.shape
    n = wt.shape[0]
    tm, tn = min(tm, s), min(tn, n)
    return pl.pallas_call(
        _matmul_t_kernel,
        grid=(n // tn, b, s // tm),
        in_specs=[pl.BlockSpec((tn, k), lambda j, bi, i: (j, 0)),
                  pl.BlockSpec((1, tm, k), lambda j, bi, i: (bi, i, 0))],
        out_specs=pl.BlockSpec((1, tn, tm), lambda j, bi, i: (bi, j, i)),
        out_shape=jax.ShapeDtypeStruct((b, n, s), out_dtype),
        compiler_params=_cparams("parallel", "parallel", "parallel"),
        name="matmul_t",
    )(wt, x)


def _outproj_kernel(x_ref, w_ref, h_ref, g_ref, hnew_ref, hn_ref):
    y = jnp.dot(x_ref[...], w_ref[...], preferred_element_type=jnp.float32)
    hnew = h_ref[...] + y
    hnew_ref[...] = hnew
    ms = jnp.mean(hnew * hnew, axis=-1, keepdims=True)
    hn_ref[...] = (hnew * lax.rsqrt(ms + EPS) * g_ref[...]).astype(hn_ref.dtype)


def _outproj(x, w, h, g_next, hn_dtype, tm=512):
    m, k = x.shape
    d = w.shape[1]
    return pl.pallas_call(
        _outproj_kernel,
        grid=(m // tm,),
        in_specs=[pl.BlockSpec((tm, k), lambda i: (i, 0)),
                  pl.BlockSpec((k, d), lambda i: (0, 0), pipeline_mode=pl.Buffered(1)),
                  pl.BlockSpec((tm, d), lambda i: (i, 0)),
                  pl.BlockSpec((1, d), lambda i: (0, 0))],
        out_specs=[pl.BlockSpec((tm, d), lambda i: (i, 0)),
                   pl.BlockSpec((tm, d), lambda i: (i, 0))],
        out_shape=[jax.ShapeDtypeStruct((m, d), jnp.float32),
                   jax.ShapeDtypeStruct((m, d), hn_dtype)],
        compiler_params=_cparams("parallel"),
        name="outproj",
    )(x, w, h, g_next.reshape(1, d))


def _dsa_latent_kernel(sm_ref, qn_ref, kvn_ref, wq_ref, wukt_ref, wiq_ref,
                       ckv_ref, ckvt_ref, ik_ref, iw_ref, qa_ref, iq_ref):
    sm = sm_ref[0]
    o1, o2, o3, o4 = Q_RANK, Q_RANK + KV_RANK, Q_RANK + KV_RANK + IDX_DIM, \
        Q_RANK + KV_RANK + IDX_DIM + IDX_HEADS

    def rms(v, g):
        ms = jnp.mean(v * v, axis=-1, keepdims=True)
        return v * lax.rsqrt(ms + EPS) * g

    cq = rms(sm[:, :o1], qn_ref[...]).astype(jnp.bfloat16)
    ckv = rms(sm[:, o1:o2], kvn_ref[...])
    ckv_ref[0] = ckv.astype(jnp.bfloat16)
    ckvt_ref[0, :KV_RANK, :] = ckv.T.astype(jnp.bfloat16)
    ckvt_ref[0, KV_RANK:, :] = jnp.ones((BF16_ROWS, sm.shape[0]), jnp.bfloat16)
    ik_ref[0] = sm[:, o2:o3].astype(jnp.bfloat16)
    iw_ref[0] = sm[:, o3:o4] * (IDX_HEADS ** -0.5 * IDX_DIM ** -0.5)
    q = jnp.dot(cq, wq_ref[...], preferred_element_type=jnp.float32).astype(jnp.bfloat16)
    iq = jnp.dot(cq, wiq_ref[...], preferred_element_type=jnp.float32).astype(jnp.bfloat16)
    for h in range(N_HEADS):
        qh = q[:, h * HEAD_DIM:(h + 1) * HEAD_DIM]
        qa = jnp.dot(qh, wukt_ref[h], preferred_element_type=jnp.float32)
        qa_ref[0, h] = (qa * (HEAD_DIM ** -0.5 * LOG2E)).astype(jnp.bfloat16)
    for h in range(IDX_HEADS):
        iq_ref[0, h] = iq[:, h * IDX_DIM:(h + 1) * IDX_DIM]


def _dsa_latents(small, q_norm, kv_norm, wq, wukt, wiq, tm=512):
    b, s, ns = small.shape
    full = lambda *shape: pl.BlockSpec(shape, lambda bi, i: (0,) * len(shape))
    vrows = KV_RANK + BF16_ROWS
    return pl.pallas_call(
        _dsa_latent_kernel,
        grid=(b, s // tm),
        in_specs=[pl.BlockSpec((1, tm, ns), lambda bi, i: (bi, i, 0)),
                  full(1, Q_RANK), full(1, KV_RANK),
                  full(Q_RANK, N_HEADS * HEAD_DIM),
                  full(N_HEADS, HEAD_DIM, KV_RANK),
                  full(Q_RANK, IDX_HEADS * IDX_DIM)],
        out_specs=[pl.BlockSpec((1, tm, KV_RANK), lambda bi, i: (bi, i, 0)),
                   pl.BlockSpec((1, vrows, tm), lambda bi, i: (bi, 0, i)),
                   pl.BlockSpec((1, tm, IDX_DIM), lambda bi, i: (bi, i, 0)),
                   pl.BlockSpec((1, tm, IDX_HEADS), lambda bi, i: (bi, i, 0)),
                   pl.BlockSpec((1, N_HEADS, tm, KV_RANK), lambda bi, i: (bi, 0, i, 0)),
                   pl.BlockSpec((1, IDX_HEADS, tm, IDX_DIM), lambda bi, i: (bi, 0, i, 0))],
        out_shape=[jax.ShapeDtypeStruct((b, s, KV_RANK), jnp.bfloat16),
                   jax.ShapeDtypeStruct((b, vrows, s), jnp.bfloat16),
                   jax.ShapeDtypeStruct((b, s, IDX_DIM), jnp.bfloat16),
                   jax.ShapeDtypeStruct((b, s, IDX_HEADS), jnp.float32),
                   jax.ShapeDtypeStruct((b, N_HEADS, s, KV_RANK), jnp.bfloat16),
                   jax.ShapeDtypeStruct((b, IDX_HEADS, s, IDX_DIM), jnp.bfloat16)],
        compiler_params=_cparams("parallel", "parallel"),
        name="dsa_latents",
    )(small, q_norm.reshape(1, -1), kv_norm.reshape(1, -1), wq, wukt, wiq)


def _t5_bucket_np(dist):
    max_exact = N_BUCKETS // 2
    d = np.maximum(dist, 0)
    df = np.maximum(d, 1).astype(np.float32)
    large = max_exact + (np.log(df / np.float32(max_exact)) / np.float32(math.log(MAX_DISTANCE / max_exact))
                         * np.float32(N_BUCKETS - max_exact)).astype(np.int32)
    large = np.minimum(large, N_BUCKETS - 1)
    return np.where(d < max_exact, d, large).astype(np.int32)


def _bias_table_kernel(bucket_ref, rb_ref, o_ref):
    h = pl.program_id(1)
    bk = bucket_ref[0]
    far = rb_ref[N_BUCKETS - 1, h]
    acc = jnp.zeros(bk.shape, jnp.float32)
    for b in range(N_BUCKETS - 1):
        acc = jnp.where(bk == b, rb_ref[b, h] - far, acc)
    o_ref[0, 0] = acc * LOG2E


def _bias_tables(rel_bias, tq, tk):
    n_slots = 2 * tk // tq
    j = np.arange(tk)[:, None]
    r = np.arange(tq)[None, :]
    buckets = np.stack([_t5_bucket_np(sl * tq + r - j) for sl in range(n_slots)])
    return pl.pallas_call(
        _bias_table_kernel,
        grid=(n_slots, N_HEADS),
        in_specs=[pl.BlockSpec((1, tk, tq), lambda sl, h: (sl, 0, 0)),
                  pl.BlockSpec(memory_space=pltpu.SMEM)],
        out_specs=pl.BlockSpec((1, 1, tk, tq), lambda sl, h: (sl, h, 0, 0)),
        out_shape=jax.ShapeDtypeStruct((n_slots, N_HEADS, tk, tq), jnp.float32),
        compiler_params=_cparams("parallel", "parallel"),
        name="t5_bias_tables",
    )(jnp.asarray(buckets), rel_bias)


def _dsa_attn_kernel(iq_ref, iwt_ref, ik_ref, ckv_ref, ckvt_ref, qa_ref, gate_ref, wuvt_ref, tbl_ref,
                     o_ref,
                     key_ref, hi_ref, acc_ref, m_ref,
                     *, topk):
    tq, tk = DSA_TQ, DSA_TK
    i = pl.program_id(1)
    t0 = i * tq
    kd = t0 // tk
    n_kt = kd + 1
    par = i % (tk // tq)

    sw = 2 * tk
    iq_all = iq_ref[0].reshape(IDX_HEADS * tq, IDX_DIM)

    def score_tile(kt, carry):
        k0 = pl.multiple_of(kt * sw, sw)
        d = lax.dot_general(ik_ref[0, pl.ds(k0, sw), :], iq_all, _NT,
                            preferred_element_type=jnp.float32)
        acc = jnp.zeros((sw, tq), jnp.float32)
        for h in range(IDX_HEADS):
            acc = acc + jnp.maximum(d[:, h * tq:(h + 1) * tq], 0.0) * iwt_ref[0, h:h + 1, :]
        bits = lax.bitcast_convert_type(acc + 0.0, jnp.int32)
        key = jnp.where(bits < 0, bits ^ jnp.int32(0x7FFFFFFF), bits)
        srow = k0 + lax.broadcasted_iota(jnp.int32, (sw, tq), 0)
        tcol = t0 + lax.broadcasted_iota(jnp.int32, (sw, tq), 1)
        causal = srow <= tcol
        key_ref[pl.ds(k0, sw), :] = jnp.where(causal, key, jnp.int32(INT_MIN))
        top = lax.bitcast_convert_type(bits & jnp.int32(-65536), jnp.float32)
        hi_ref[pl.ds(k0, sw), :] = jnp.where(causal, top, jnp.nan).astype(jnp.bfloat16)
        return carry

    lax.fori_loop(0, (n_kt + 1) // 2, score_tile, 0)

    kf = jnp.float32(topk)
    n_wt = (n_kt + 1) // 2

    def tree_sum(ind, rows):
        parts = [ind[j * rows:(j + 1) * rows, :] for j in range(ind.shape[0] // rows)]
        while len(parts) > 1:
            parts = [parts[j] + parts[j + 1] for j in range(0, len(parts), 2)]
        return parts[0]

    def count16(candf):
        one, nil = jnp.ones((), jnp.bfloat16), jnp.zeros((), jnp.bfloat16)

        def body(wt, c):
            k0 = pl.multiple_of(wt * sw, sw)
            ind = jnp.where(hi_ref[pl.ds(k0, sw), :] >= candf, one, nil)
            return c + tree_sum(ind, BF16_ROWS)

        c = lax.fori_loop(0, n_wt, body, jnp.zeros((BF16_ROWS, tq), jnp.bfloat16))
        return jnp.sum(c.astype(jnp.float32), axis=0, keepdims=True)

    def count32(cand):
        def body(kt, c):
            k0 = pl.multiple_of(kt * tk, tk)
            ind = jnp.where(key_ref[pl.ds(k0, tk), :] >= cand, 1.0, 0.0)
            return c + tree_sum(ind, SUBLANES)

        c = lax.fori_loop(0, n_kt, body, jnp.zeros((SUBLANES, tq), jnp.float32))
        return jnp.sum(c, axis=0, keepdims=True)

    def bf16_of_pattern(pat):
        return lax.bitcast_convert_type(jnp.left_shift(pat, 16), jnp.float32).astype(jnp.bfloat16)

    def hi_candidate(c16):
        pat = jnp.where(c16 < 0, c16 ^ jnp.int32(0x7FFF), c16) & jnp.int32(0xFFFF)
        subnormal = (pat > 0) & (pat < jnp.int32(0x0080))
        return bf16_of_pattern(jnp.where(subnormal, jnp.int32(0x0080), pat))

    zero = jnp.zeros((1, tq), jnp.int32)
    h0 = jnp.where(count16(hi_candidate(zero)) >= kf, zero, jnp.int32(-32768))

    def search_hi(it, h):
        cand = h + jnp.left_shift(jnp.int32(1), jnp.int32(14) - it)
        return jnp.where(count16(hi_candidate(cand)) >= kf, cand, h)

    h16 = lax.fori_loop(0, 15, search_hi, h0)

    lo_base = jnp.int32(0x3000)

    def build_lo(wt, carry):
        k0 = pl.multiple_of(wt * sw, sw)
        kk = key_ref[pl.ds(k0, sw), :]
        e_hi = jnp.right_shift(kk, 16)
        pat = lo_base + jnp.right_shift(kk & jnp.int32(0xFFFF), 2)
        inside = lax.bitcast_convert_type(jnp.left_shift(pat, 16), jnp.float32)
        v = jnp.where(e_hi > h16, jnp.inf, jnp.where(e_hi == h16, inside, -1.0))
        hi_ref[pl.ds(k0, sw), :] = v.astype(jnp.bfloat16)
        return carry

    lax.fori_loop(0, n_wt, build_lo, 0)

    def search_lo(it, a):
        cand = a + jnp.left_shift(jnp.int32(1), jnp.int32(13) - it)
        return jnp.where(count16(bf16_of_pattern(lo_base + cand)) >= kf, cand, a)

    a14 = lax.fori_loop(0, 14, search_lo, zero)
    thr = jnp.left_shift(h16, 16) + jnp.left_shift(a14, 2)
    for bit in (1, 0):
        cand = thr + jnp.int32(1 << bit)
        thr = jnp.where(count32(cand) >= kf, cand, thr)
    thr = jnp.maximum(thr, jnp.int32(INT_MIN + 1))

    def mask_tile(kt, carry):
        k0 = pl.multiple_of(kt * tk, tk)
        mb = jnp.where(key_ref[pl.ds(k0, tk), :] >= thr, 0.0, MASKED)
        key_ref[pl.ds(k0, tk), :] = lax.bitcast_convert_type(mb, jnp.int32)
        return carry

    lax.fori_loop(0, n_kt, mask_tile, 0)

    m_ref[...] = jnp.full(m_ref.shape, MASKED, jnp.float32)
    acc_ref[...] = jnp.zeros(acc_ref.shape, jnp.float32)

    qa = qa_ref[0].reshape(N_HEADS * tq, KV_RANK)

    def attn_tile(k0, width, slot):
        ck = ckv_ref[0, pl.ds(k0, width), :]
        ckt = ckvt_ref[0, :, pl.ds(k0, width)]
        mbt = lax.bitcast_convert_type(key_ref[pl.ds(k0, width), :], jnp.float32)
        st = lax.dot_general(ck, qa, _NT, preferred_element_type=jnp.float32)
        ps, als = [], []
        for h in range(N_HEADS):
            cols = slice(h * tq, (h + 1) * tq)
            s = st[:, cols] + mbt
            if slot is not None:
                s = s + tbl_ref[slot, h]
            m_prev = m_ref[:, cols]
            m_new = jnp.maximum(m_prev, _colmax(s))
            m_ref[:, cols] = m_new
            ps.append(jnp.exp2(s - m_new).astype(jnp.bfloat16))
            als.append(jnp.exp2(m_prev - m_new))
        pv = jnp.dot(ckt, jnp.concatenate(ps, axis=1), preferred_element_type=jnp.float32)
        acc_ref[...] = acc_ref[...] * jnp.concatenate(als, axis=1) + pv

    n_far = jnp.maximum(kd - 1, 0)
    n_wide = n_far // 2

    def wide_tile(p, carry):
        attn_tile(pl.multiple_of(p * 2 * tk, 2 * tk), 2 * tk, None)
        return carry

    lax.fori_loop(0, n_wide, wide_tile, 0)

    @pl.when(n_far % 2 == 1)
    def _():
        attn_tile(pl.multiple_of(n_wide * 2 * tk, tk), tk, None)

    @pl.when(kd >= 1)
    def _():
        attn_tile(pl.multiple_of((kd - 1) * tk, tk), tk, par + tk // tq)

    attn_tile(pl.multiple_of(kd * tk, tk), tk, par)

    for h in range(N_HEADS):
        a = acc_ref[:, h * tq:(h + 1) * tq]
        o_lat_t = (a[:KV_RANK] / a[KV_RANK:KV_RANK + 1]).astype(jnp.bfloat16)
        o = jnp.dot(wuvt_ref[h], o_lat_t, preferred_element_type=jnp.float32).T
        g = gate_ref[0, :, h * HEAD_DIM:(h + 1) * HEAD_DIM].astype(jnp.float32)
        o_ref[0, :, h * HEAD_DIM:(h + 1) * HEAD_DIM] = (o * (g * jax.nn.sigmoid(g))).astype(o_ref.dtype)


def _dsa_attention(iq, iwt, ik, ckv, ckvt, qa, gate, wuvt, tbl, topk):
    b, s, _ = ckv.shape
    tq, tk = DSA_TQ, DSA_TK
    d_inner = N_HEADS * HEAD_DIM
    vrows = ckvt.shape[1]
    const = lambda *shape: pl.BlockSpec(shape, lambda bi, i: (0,) * len(shape))
    return pl.pallas_call(
        functools.partial(_dsa_attn_kernel, topk=topk),
        grid=(b, s // tq),
        in_specs=[pl.BlockSpec((1, IDX_HEADS, tq, IDX_DIM), lambda bi, i: (bi, 0, i, 0)),
                  pl.BlockSpec((1, IDX_HEADS, tq), lambda bi, i: (bi, 0, i)),
                  pl.BlockSpec((1, s, IDX_DIM), lambda bi, i: (bi, 0, 0)),
                  pl.BlockSpec((1, s, KV_RANK), lambda bi, i: (bi, 0, 0)),
                  pl.BlockSpec((1, vrows, s), lambda bi, i: (bi, 0, 0)),
                  pl.BlockSpec((1, N_HEADS, tq, KV_RANK), lambda bi, i: (bi, 0, i, 0)),
                  pl.BlockSpec((1, tq, d_inner), lambda bi, i: (bi, i, 0)),
                  const(N_HEADS, HEAD_DIM, KV_RANK),
                  const(*tbl.shape)],
        out_specs=pl.BlockSpec((1, tq, d_inner), lambda bi, i: (bi, i, 0)),
        out_shape=jax.ShapeDtypeStruct((b, s, d_inner), jnp.bfloat16),
        scratch_shapes=[pltpu.VMEM((s, tq), jnp.int32),
                        pltpu.VMEM((s, tq), jnp.bfloat16),
                        pltpu.VMEM((vrows, N_HEADS * tq), jnp.float32),
                        pltpu.VMEM((1, N_HEADS * tq), jnp.float32)],
        compiler_params=_cparams("parallel", "arbitrary"),
        name="dsa_attention",
    )(iq, iwt, ik, ckv, ckvt, qa, gate, wuvt, tbl)


def _fox_decay_kernel(f_ref, fb_ref, place_ref, o_ref, carry_ref):
    j = pl.program_id(1)

    @pl.when(j == 0)
    def _():
        carry_ref[...] = jnp.zeros(carry_ref.shape, jnp.float32)

    x = f_ref[0][:, :N_HEADS] + fb_ref[...]
    lf = jnp.minimum(x, 0.0) - jnp.log(1.0 + jnp.exp(-jnp.abs(x)))
    chunk = x.shape[0]
    r = lax.broadcasted_iota(jnp.int32, (chunk, chunk), 0)
    c = lax.broadcasted_iota(jnp.int32, (chunk, chunk), 1)
    tri = jnp.where(c <= r, 1.0, 0.0).astype(jnp.bfloat16)
    cs = carry_ref[0:1, :N_HEADS]
    for piece in _split_bf16(lf, 3):
        cs = cs + jnp.dot(tri, piece, preferred_element_type=jnp.float32)
    carry_ref[0:1, :N_HEADS] = cs[chunk - 1:chunk, :]
    out = jnp.zeros(o_ref.shape[1:], jnp.float32)
    for ci, piece in enumerate(_split_bf16(cs * (-LOG2E), FOX_DK_PIECES)):
        out = out + jnp.dot(piece, place_ref[ci], preferred_element_type=jnp.float32)
    o_ref[0] = out.astype(o_ref.dtype)


def _fox_decay(f_pre, f_bias, chunk=512):
    b, s, fw = f_pre.shape
    chunk = min(chunk, s)
    width = N_HEADS // FOX_HG * LANES
    heads = np.arange(N_HEADS)
    place = np.zeros((FOX_DK_PIECES, N_HEADS, width), np.float32)
    for ci in range(FOX_DK_PIECES):
        place[ci, heads, heads // FOX_HG * LANES + heads % FOX_HG * FOX_DK_PIECES + ci] = 1.0
    return pl.pallas_call(
        _fox_decay_kernel,
        grid=(b, s // chunk),
        in_specs=[pl.BlockSpec((1, chunk, fw), lambda bi, j: (bi, j, 0)),
                  pl.BlockSpec((1, N_HEADS), lambda bi, j: (0, 0)),
                  pl.BlockSpec((FOX_DK_PIECES, N_HEADS, width), lambda bi, j: (0, 0, 0))],
        out_specs=pl.BlockSpec((1, chunk, width), lambda bi, j: (bi, j, 0)),
        out_shape=jax.ShapeDtypeStruct((b, s, width), jnp.bfloat16),
        scratch_shapes=[pltpu.VMEM((SUBLANES, LANES), jnp.float32)],
        compiler_params=_cparams("parallel", "arbitrary"),
        name="fox_decay",
    )(f_pre, f_bias.reshape(1, N_HEADS), jnp.asarray(place, jnp.bfloat16))


def _fox_attn_kernel(q_ref, k_ref, kx_ref, vt_ref, gate_ref, o_ref, m_ref, acc_ref, *, tq, tk):
    i = pl.program_id(2)
    m_ref[...] = jnp.full(m_ref.shape, MASKED, jnp.float32)
    acc_ref[...] = jnp.zeros(acc_ref.shape, jnp.float32)
    lane = lax.broadcasted_iota(jnp.int32, (tq, LANES), 1)
    qxs = [jnp.where((lane >= hh * FOX_DK_PIECES) & (lane < (hh + 1) * FOX_DK_PIECES), 1.0, 0.0
                     ).astype(jnp.bfloat16) for hh in range(FOX_HG)]
    ones = jnp.ones((BF16_ROWS, tk), jnp.bfloat16)

    def tile(kt, masked):
        k0 = pl.multiple_of(kt * tk, tk)

        def scores_t(hh):
            cols = slice(hh * HEAD_DIM, (hh + 1) * HEAD_DIM)
            qc = jnp.concatenate([q_ref[0, :, cols], qxs[hh]], axis=1)
            kc = jnp.concatenate([k_ref[0, pl.ds(k0, tk), cols],
                                  kx_ref[0, pl.ds(k0, tk), :]], axis=1)
            return lax.dot_general(kc, qc, _NT, preferred_element_type=jnp.float32)

        sts = [scores_t(0)]
        for hh in range(FOX_HG):
            if hh + 1 < FOX_HG:
                sts.append(scores_t(hh + 1))
            cols = slice(hh * HEAD_DIM, (hh + 1) * HEAD_DIM)
            s = sts[hh]
            if masked:
                srow = lax.broadcasted_iota(jnp.int32, (tk, tq), 0)
                tcol = lax.broadcasted_iota(jnp.int32, (tk, tq), 1)
                s = jnp.where(srow <= tcol, s, MASKED)
            m_prev = m_ref[hh]
            m_new = jnp.maximum(m_prev, _colmax(s))
            alpha = jnp.exp2(m_prev - m_new)
            p = jnp.exp2(s - m_new).astype(jnp.bfloat16)
            m_ref[hh] = m_new
            vt = jnp.concatenate([vt_ref[0, cols, pl.ds(k0, tk)], ones], axis=0)
            acc_ref[hh] = acc_ref[hh] * alpha + jnp.dot(vt, p, preferred_element_type=jnp.float32)

    def full_tile(kt, carry):
        tile(kt, False)
        return carry

    lax.fori_loop(0, i, full_tile, 0)
    tile(i, True)

    for hh in range(FOX_HG):
        cols = slice(hh * HEAD_DIM, (hh + 1) * HEAD_DIM)
        a = acc_ref[hh]
        o = (a[:HEAD_DIM] / a[HEAD_DIM:HEAD_DIM + 1]).T
        g = gate_ref[0, :, cols].astype(jnp.float32)
        o_ref[0, :, cols] = (o * (g * jax.nn.sigmoid(g))).astype(o_ref.dtype)


def _fox_attention(proj, kx, vt, d_inner):
    b, s, _ = proj.shape
    tq = tk = min(FOX_TQ, s)
    nh = d_inner // HEAD_DIM
    ng = nh // FOX_HG
    w = FOX_HG * HEAD_DIM
    return pl.pallas_call(
        functools.partial(_fox_attn_kernel, tq=tq, tk=tk),
        grid=(b, ng, s // tq),
        in_specs=[pl.BlockSpec((1, tq, w), lambda bi, h, i: (bi, i, h)),
                  pl.BlockSpec((1, s, w), lambda bi, h, i: (bi, 0, ng + h)),
                  pl.BlockSpec((1, s, LANES), lambda bi, h, i: (bi, 0, h)),
                  pl.BlockSpec((1, w, s), lambda bi, h, i: (bi, h, 0)),
                  pl.BlockSpec((1, tq, w), lambda bi, h, i: (bi, i, 2 * ng + h))],
        out_specs=pl.BlockSpec((1, tq, w), lambda bi, h, i: (bi, i, h)),
        out_shape=jax.ShapeDtypeStruct((b, s, d_inner), jnp.bfloat16),
        scratch_shapes=[pltpu.VMEM((FOX_HG, 1, tq), jnp.float32),
                        pltpu.VMEM((FOX_HG, HEAD_DIM + BF16_ROWS, tq), jnp.float32)],
        compiler_params=_cparams("parallel", "parallel", "arbitrary"),
        name="fox_attention",
    )(proj, proj, kx, vt, proj)


def _dsa_layer(hn, h, g_next, hn_dtype, b, s, w_in, q_norm, kv_norm, w_q_up, w_uk, w_uv, w_iq, w_out, tbl):
    bf = jnp.bfloat16
    d_inner = N_HEADS * HEAD_DIM
    n_small = Q_RANK + KV_RANK + IDX_DIM + IDX_HEADS
    n_pad = -(-n_small // LANES) * LANES
    w_small = jnp.pad(w_in[:, :n_small], ((0, 0), (0, n_pad - n_small))).astype(bf)
    w_gate = w_in[:, n_small:].astype(bf)
    small = _matmul(hn, w_small, jnp.ones((n_pad,), jnp.float32), jnp.float32, 1024, n_pad)
    gate = _matmul(hn, w_gate, jnp.ones((d_inner,), jnp.float32), bf, 1024, 1024)
    wukt = jnp.transpose(w_uk, (1, 2, 0)).astype(bf)
    wuvt = jnp.transpose(w_uv, (1, 2, 0)).astype(bf)
    ckv, ckvt, ik, iw, qa, iq = _dsa_latents(small.reshape(b, s, n_pad), q_norm, kv_norm,
                                             w_q_up.astype(bf), wukt, w_iq.astype(bf),
                                             tm=min(512, s))
    iwt = jnp.transpose(iw, (0, 2, 1))
    topk = min(TOPK_MAX, s // 4)
    og = _dsa_attention(iq, iwt, ik, ckv, ckvt, qa, gate.reshape(b, s, d_inner), wuvt, tbl, topk)
    return _outproj(og.reshape(b * s, d_inner), w_out.astype(bf), h, g_next, hn_dtype)


def _fox_layer(hn, h, g_next, hn_dtype, b, s, w_in, f_bias, w_out):
    bf = jnp.bfloat16
    d_inner = N_HEADS * HEAD_DIM
    w_qkg = jnp.concatenate([w_in[:, :2 * d_inner], w_in[:, 3 * d_inner + N_HEADS:]], axis=1).astype(bf)
    w_vt = jnp.transpose(w_in[:, 2 * d_inner:3 * d_inner]).astype(bf)
    w_f = jnp.pad(w_in[:, 3 * d_inner:3 * d_inner + N_HEADS], ((0, 0), (0, LANES - N_HEADS))).astype(bf)
    colscale = jnp.concatenate([jnp.full((d_inner,), HEAD_DIM ** -0.5 * LOG2E, jnp.float32),
                                jnp.ones((2 * d_inner,), jnp.float32)])
    proj = _matmul(hn, w_qkg, colscale, bf, 1024, 1024)
    vt = _matmul_t(hn.reshape(b, s, -1), w_vt, bf, 1024, 1024)
    f_pre = _matmul(hn, w_f, jnp.ones((LANES,), jnp.float32), jnp.float32, 1024, LANES)
    kx = _fox_decay(f_pre.reshape(b, s, LANES), f_bias)
    og = _fox_attention(proj.reshape(b, s, 3 * d_inner), kx, vt, d_inner)
    return _outproj(og.reshape(b * s, d_inner), w_out.astype(bf), h, g_next, hn_dtype)


def kernel(x, norm_g, final_g, rel_bias, a_w_in, a_q_norm, a_kv_norm, a_w_q_up, a_w_uk, a_w_uv,
           a_w_iq, a_w_out, b_w_in, b_f_bias, b_w_out):
    b, s, d = x.shape
    depth = norm_g.shape[0]
    tbl = _bias_tables(rel_bias, DSA_TQ, DSA_TK)
    h = x.reshape(b * s, d)
    hn = _rmsnorm(h, norm_g[0], jnp.bfloat16)
    for i in range(depth):
        last = i == depth - 1
        g_next = final_g if last else norm_g[i + 1]
        hn_dtype = x.dtype if last else jnp.bfloat16
        j = i // 2
        if i % 2 == 0:
            h, hn = _dsa_layer(hn, h, g_next, hn_dtype, b, s, a_w_in[j], a_q_norm[j], a_kv_norm[j],
                               a_w_q_up[j], a_w_uk[j], a_w_uv[j], a_w_iq[j], a_w_out[j], tbl)
        else:
            h, hn = _fox_layer(hn, h, g_next, hn_dtype, b, s, b_w_in[j], b_f_bias[j], b_w_out[j])
    return hn.reshape(b, s, d)
```

```python
import functools
import math

import numpy as np
import jax
import jax.numpy as jnp
from jax import lax
from jax.experimental import pallas as pl
from jax.experimental.pallas import tpu as pltpu

N_HEADS = 16
HEAD_DIM = 128
Q_RANK = 512
KV_RANK = 256
IDX_HEADS = 16
IDX_DIM = 64
TOPK_MAX = 256
N_BUCKETS = 32
MAX_DISTANCE = 128
EPS = 1e-6

LOG2E = 1.4426950408889634
MASKED = -1e30
INT_MIN = -2147483648
LANES = 128
SUBLANES = 8
BF16_ROWS = 16
VMEM_LIMIT = 48 * 1024 * 1024

DSA_TQ = 128
DSA_TK = 256
DSA_HG = 16
FOX_TQ = 512
FOX_TK = 512
FOX_HG = 4
FOX_DK_PIECES = 3

_NT = (((1,), (1,)), ((), ()))


def _cparams(*sem):
    return pltpu.CompilerParams(dimension_semantics=sem, vmem_limit_bytes=VMEM_LIMIT)


def _colmax(s):
    parts = [s[j * SUBLANES:(j + 1) * SUBLANES, :] for j in range(s.shape[0] // SUBLANES)]
    while len(parts) > 1:
        parts = [jnp.maximum(parts[j], parts[j + 1]) for j in range(0, len(parts), 2)]
    return jnp.max(parts[0], axis=0, keepdims=True)


def _split_bf16(x, n):
    pieces = []
    for _ in range(n):
        p = x.astype(jnp.bfloat16)
        pieces.append(p)
        x = x - p.astype(jnp.float32)
    return pieces


def _rmsnorm_kernel(x_ref, g_ref, o_ref):
    x = x_ref[...]
    ms = jnp.mean(x * x, axis=-1, keepdims=True)
    o_ref[...] = (x * lax.rsqrt(ms + EPS) * g_ref[...]).astype(o_ref.dtype)


def _rmsnorm(x2d, g, out_dtype, tm=512):
    m, d = x2d.shape
    return pl.pallas_call(
        _rmsnorm_kernel,
        grid=(m // tm,),
        in_specs=[pl.BlockSpec((tm, d), lambda i: (i, 0)),
                  pl.BlockSpec((1, d), lambda i: (0, 0))],
        out_specs=pl.BlockSpec((tm, d), lambda i: (i, 0)),
        out_shape=jax.ShapeDtypeStruct((m, d), out_dtype),
        compiler_params=_cparams("parallel"),
        name="rmsnorm",
    )(x2d, g.reshape(1, d))


def _matmul_kernel(x_ref, w_ref, cs_ref, o_ref):
    acc = jnp.dot(x_ref[...], w_ref[...], preferred_element_type=jnp.float32)
    o_ref[...] = (acc * cs_ref[...]).astype(o_ref.dtype)


def _matmul(x, w, colscale, out_dtype, tm, tn):
    m, k = x.shape
    n = w.shape[1]
    tm, tn = min(tm, m), min(tn, n)
    return pl.pallas_call(
        _matmul_kernel,
        grid=(n // tn, m // tm),
        in_specs=[pl.BlockSpec((tm, k), lambda j, i: (i, 0)),
                  pl.BlockSpec((k, tn), lambda j, i: (0, j)),
                  pl.BlockSpec((1, tn), lambda j, i: (0, j))],
        out_specs=pl.BlockSpec((tm, tn), lambda j, i: (i, j)),
        out_shape=jax.ShapeDtypeStruct((m, n), out_dtype),
        compiler_params=_cparams("parallel", "parallel"),
        name="matmul",
    )(x, w, colscale.reshape(1, n))


def _matmul_t_kernel(wt_ref, x_ref, o_ref):
    o_ref[0] = lax.dot_general(wt_ref[...], x_ref[0], _NT,
                               preferred_element_type=jnp.float32).astype(o_ref.dtype)


def _matmul_t(x, wt, out_dtype, tm, tn):
    b, s, k = x.shape
    n = wt.shape[0]
    tm, tn = min(tm, s), min(tn, n)
    return pl.pallas_call(
        _matmul_t_kernel,
        grid=(n // tn, b, s // tm),
        in_specs=[pl.BlockSpec((tn, k), lambda j, bi, i: (j, 0)),
                  pl.BlockSpec((1, tm, k), lambda j, bi, i: (bi, i, 0))],
        out_specs=pl.BlockSpec((1, tn, tm), lambda j, bi, i: (bi, j, i)),
        out_shape=jax.ShapeDtypeStruct((b, n, s), out_dtype),
        compiler_params=_cparams("parallel", "parallel", "parallel"),
        name="matmul_t",
    )(wt, x)


def _outproj_kernel(x_ref, w_ref, h_ref, g_ref, hnew_ref, hn_ref):
    y = jnp.dot(x_ref[...], w_ref[...], preferred_element_type=jnp.float32)
    hnew = h_ref[...] + y
    hnew_ref[...] = hnew
    ms = jnp.mean(hnew * hnew, axis=-1, keepdims=True)
    hn_ref[...] = (hnew * lax.rsqrt(ms + EPS) * g_ref[...]).astype(hn_ref.dtype)


def _outproj(x, w, h, g_next, hn_dtype, tm=512):
    m, k = x.shape
    d = w.shape[1]
    return pl.pallas_call(
        _outproj_kernel,
        grid=(m // tm,),
        in_specs=[pl.BlockSpec((tm, k), lambda i: (i, 0)),
                  pl.BlockSpec((k, d), lambda i: (0, 0), pipeline_mode=pl.Buffered(1)),
                  pl.BlockSpec((tm, d), lambda i: (i, 0)),
                  pl.BlockSpec((1, d), lambda i: (0, 0))],
        out_specs=[pl.BlockSpec((tm, d), lambda i: (i, 0)),
                   pl.BlockSpec((tm, d), lambda i: (i, 0))],
        out_shape=[jax.ShapeDtypeStruct((m, d), jnp.float32),
                   jax.ShapeDtypeStruct((m, d), hn_dtype)],
        compiler_params=_cparams("parallel"),
        name="outproj",
    )(x, w, h, g_next.reshape(1, d))


def _dsa_latent_kernel(sm_ref, qn_ref, kvn_ref, wq_ref, wukt_ref, wiq_ref,
                       ckv_ref, ckvt_ref, ik_ref, iw_ref, qa_ref, iq_ref):
    sm = sm_ref[0]
    o1, o2, o3, o4 = Q_RANK, Q_RANK + KV_RANK, Q_RANK + KV_RANK + IDX_DIM, \
        Q_RANK + KV_RANK + IDX_DIM + IDX_HEADS

    def rms(v, g):
        ms = jnp.mean(v * v, axis=-1, keepdims=True)
        return v * lax.rsqrt(ms + EPS) * g

    cq = rms(sm[:, :o1], qn_ref[...]).astype(jnp.bfloat16)
    ckv = rms(sm[:, o1:o2], kvn_ref[...])
    ckv_ref[0] = ckv.astype(jnp.bfloat16)
    ckvt_ref[0, :KV_RANK, :] = ckv.T.astype(jnp.bfloat16)
    ckvt_ref[0, KV_RANK:, :] = jnp.ones((BF16_ROWS, sm.shape[0]), jnp.bfloat16)
    ik_ref[0] = sm[:, o2:o3].astype(jnp.bfloat16)
    iw_ref[0] = sm[:, o3:o4] * (IDX_HEADS ** -0.5 * IDX_DIM ** -0.5)
    q = jnp.dot(cq, wq_ref[...], preferred_element_type=jnp.float32).astype(jnp.bfloat16)
    iq = jnp.dot(cq, wiq_ref[...], preferred_element_type=jnp.float32).astype(jnp.bfloat16)
    for h in range(N_HEADS):
        qh = q[:, h * HEAD_DIM:(h + 1) * HEAD_DIM]
        qa = jnp.dot(qh, wukt_ref[h], preferred_element_type=jnp.float32)
        qa_ref[0, h] = (qa * (HEAD_DIM ** -0.5 * LOG2E)).astype(jnp.bfloat16)
    for h in range(IDX_HEADS):
        iq_ref[0, h] = iq[:, h * IDX_DIM:(h + 1) * IDX_DIM]


def _dsa_latents(small, q_norm, kv_norm, wq, wukt, wiq, tm=512):
    b, s, ns = small.shape
    full = lambda *shape: pl.BlockSpec(shape, lambda bi, i: (0,) * len(shape))
    vrows = KV_RANK + BF16_ROWS
    return pl.pallas_call(
        _dsa_latent_kernel,
        grid=(b, s // tm),
        in_specs=[pl.BlockSpec((1, tm, ns), lambda bi, i: (bi, i, 0)),
                  full(1, Q_RANK), full(1, KV_RANK),
                  full(Q_RANK, N_HEADS * HEAD_DIM),
                  full(N_HEADS, HEAD_DIM, KV_RANK),
                  full(Q_RANK, IDX_HEADS * IDX_DIM)],
        out_specs=[pl.BlockSpec((1, tm, KV_RANK), lambda bi, i: (bi, i, 0)),
                   pl.BlockSpec((1, vrows, tm), lambda bi, i: (bi, 0, i)),
                   pl.BlockSpec((1, tm, IDX_DIM), lambda bi, i: (bi, i, 0)),
                   pl.BlockSpec((1, tm, IDX_HEADS), lambda bi, i: (bi, i, 0)),
                   pl.BlockSpec((1, N_HEADS, tm, KV_RANK), lambda bi, i: (bi, 0, i, 0)),
                   pl.BlockSpec((1, IDX_HEADS, tm, IDX_DIM), lambda bi, i: (bi, 0, i, 0))],
        out_shape=[jax.ShapeDtypeStruct((b, s, KV_RANK), jnp.bfloat16),
                   jax.ShapeDtypeStruct((b, vrows, s), jnp.bfloat16),
                   jax.ShapeDtypeStruct((b, s, IDX_DIM), jnp.bfloat16),
                   jax.ShapeDtypeStruct((b, s, IDX_HEADS), jnp.float32),
                   jax.ShapeDtypeStruct((b, N_HEADS, s, KV_RANK), jnp.bfloat16),
                   jax.ShapeDtypeStruct((b, IDX_HEADS, s, IDX_DIM), jnp.bfloat16)],
        compiler_params=_cparams("parallel", "parallel"),
        name="dsa_latents",
    )(small, q_norm.reshape(1, -1), kv_norm.reshape(1, -1), wq, wukt, wiq)


def _t5_bucket_np(dist):
    max_exact = N_BUCKETS // 2
    d = np.maximum(dist, 0)
    df = np.maximum(d, 1).astype(np.float32)
    large = max_exact + (np.log(df / np.float32(max_exact)) / np.float32(math.log(MAX_DISTANCE / max_exact))
                         * np.float32(N_BUCKETS - max_exact)).astype(np.int32)
    large = np.minimum(large, N_BUCKETS - 1)
    return np.where(d < max_exact, d, large).astype(np.int32)


def _bias_table_kernel(bucket_ref, rb_ref, o_ref):
    h = pl.program_id(1)
    bk = bucket_ref[0]
    far = rb_ref[N_BUCKETS - 1, h]
    acc = jnp.zeros(bk.shape, jnp.float32)
    for b in range(N_BUCKETS - 1):
        acc = jnp.where(bk == b, rb_ref[b, h] - far, acc)
    o_ref[0, 0] = acc * LOG2E


def _bias_tables(rel_bias, tq, tk):
    n_slots = 2 * tk // tq
    j = np.arange(tk)[:, None]
    r = np.arange(tq)[None, :]
    buckets = np.stack([_t5_bucket_np(sl * tq + r - j) for sl in range(n_slots)])
    return pl.pallas_call(
        _bias_table_kernel,
        grid=(n_slots, N_HEADS),
        in_specs=[pl.BlockSpec((1, tk, tq), lambda sl, h: (sl, 0, 0)),
                  pl.BlockSpec(memory_space=pltpu.SMEM)],
        out_specs=pl.BlockSpec((1, 1, tk, tq), lambda sl, h: (sl, h, 0, 0)),
        out_shape=jax.ShapeDtypeStruct((n_slots, N_HEADS, tk, tq), jnp.float32),
        compiler_params=_cparams("parallel", "parallel"),
        name="t5_bias_tables",
    )(jnp.asarray(buckets), rel_bias)


def _dsa_attn_kernel(iq_ref, iwt_ref, ik_ref, ckv_ref, ckvt_ref, qa_ref, gate_ref, wuvt_ref, tbl_ref,
                     o_ref,
                     key_ref, acc_ref, m_ref,
                     *, topk):
    tq, tk = DSA_TQ, DSA_TK
    i = pl.program_id(1)
    t0 = i * tq
    kd = t0 // tk
    n_kt = kd + 1
    par = i % (tk // tq)

    sw = 2 * tk
    iq_all = iq_ref[0].reshape(IDX_HEADS * tq, IDX_DIM)

    def score_tile(kt, carry):
        k0 = pl.multiple_of(kt * sw, sw)
        d = lax.dot_general(ik_ref[0, pl.ds(k0, sw), :], iq_all, _NT,
                            preferred_element_type=jnp.float32)
        acc = jnp.zeros((sw, tq), jnp.float32)
        for h in range(IDX_HEADS):
            acc = acc + jnp.maximum(d[:, h * tq:(h + 1) * tq], 0.0) * iwt_ref[0, h:h + 1, :]
        bits = lax.bitcast_convert_type(acc, jnp.int32)
        key = jnp.where(bits < 0, bits ^ jnp.int32(0x7FFFFFFF), bits)
        srow = k0 + lax.broadcasted_iota(jnp.int32, (sw, tq), 0)
        tcol = t0 + lax.broadcasted_iota(jnp.int32, (sw, tq), 1)
        key_ref[pl.ds(k0, sw), :] = jnp.where(srow <= tcol, key, jnp.int32(INT_MIN))
        return carry

    lax.fori_loop(0, (n_kt + 1) // 2, score_tile, 0)

    def count_ge(cand):
        def body(kt, c):
            k0 = pl.multiple_of(kt * tk, tk)
            ind = jnp.where(key_ref[pl.ds(k0, tk), :] >= cand, 1.0, 0.0)
            parts = [ind[j * SUBLANES:(j + 1) * SUBLANES, :] for j in range(tk // SUBLANES)]
            while len(parts) > 1:
                parts = [parts[j] + parts[j + 1] for j in range(0, len(parts), 2)]
            return c + parts[0]

        c = lax.fori_loop(0, n_kt, body, jnp.zeros((SUBLANES, tq), jnp.float32))
        return jnp.sum(c, axis=0, keepdims=True)

    kf = jnp.float32(topk)
    zero = jnp.zeros((1, tq), jnp.int32)
    ans0 = jnp.where(count_ge(zero) >= kf, zero, jnp.int32(INT_MIN))

    def search(it, ans):
        cand = ans + jnp.left_shift(jnp.int32(1), jnp.int32(30) - it)
        return jnp.where(count_ge(cand) >= kf, cand, ans)

    thr = lax.fori_loop(0, 31, search, ans0)
    thr = jnp.maximum(thr, jnp.int32(INT_MIN + 1))

    def mask_tile(kt, carry):
        k0 = pl.multiple_of(kt * tk, tk)
        mb = jnp.where(key_ref[pl.ds(k0, tk), :] >= thr, 0.0, MASKED)
        key_ref[pl.ds(k0, tk), :] = lax.bitcast_convert_type(mb, jnp.int32)
        return carry

    lax.fori_loop(0, n_kt, mask_tile, 0)

    m_ref[...] = jnp.full(m_ref.shape, MASKED, jnp.float32)
    acc_ref[...] = jnp.zeros(acc_ref.shape, jnp.float32)

    qa = qa_ref[0].reshape(N_HEADS * tq, KV_RANK)

    def attn_tile(k0, width, slot):
        ck = ckv_ref[0, pl.ds(k0, width), :]
        ckt = ckvt_ref[0, :, pl.ds(k0, width)]
        mbt = lax.bitcast_convert_type(key_ref[pl.ds(k0, width), :], jnp.float32)
        st = lax.dot_general(ck, qa, _NT, preferred_element_type=jnp.float32)
        ps, als = [], []
        for h in range(N_HEADS):
            cols = slice(h * tq, (h + 1) * tq)
            s = st[:, cols] + mbt
            if slot is not None:
                s = s + tbl_ref[slot, h]
            m_prev = m_ref[:, cols]
            m_new = jnp.maximum(m_prev, _colmax(s))
            m_ref[:, cols] = m_new
            ps.append(jnp.exp2(s - m_new).astype(jnp.bfloat16))
            als.append(jnp.exp2(m_prev - m_new))
        pv = jnp.dot(ckt, jnp.concatenate(ps, axis=1), preferred_element_type=jnp.float32)
        acc_ref[...] = acc_ref[...] * jnp.concatenate(als, axis=1) + pv

    n_far = jnp.maximum(kd - 1, 0)
    n_wide = n_far // 2

    def wide_tile(p, carry):
        attn_tile(pl.multiple_of(p * 2 * tk, 2 * tk), 2 * tk, None)
        return carry

    lax.fori_loop(0, n_wide, wide_tile, 0)

    @pl.when(n_far % 2 == 1)
    def _():
        attn_tile(pl.multiple_of(n_wide * 2 * tk, tk), tk, None)

    @pl.when(kd >= 1)
    def _():
        attn_tile(pl.multiple_of((kd - 1) * tk, tk), tk, par + tk // tq)

    attn_tile(pl.multiple_of(kd * tk, tk), tk, par)

    for h in range(N_HEADS):
        a = acc_ref[:, h * tq:(h + 1) * tq]
        o_lat_t = (a[:KV_RANK] / a[KV_RANK:KV_RANK + 1]).astype(jnp.bfloat16)
        o = jnp.dot(wuvt_ref[h], o_lat_t, preferred_element_type=jnp.float32).T
        g = gate_ref[0, :, h * HEAD_DIM:(h + 1) * HEAD_DIM].astype(jnp.float32)
        o_ref[0, :, h * HEAD_DIM:(h + 1) * HEAD_DIM] = (o * (g * jax.nn.sigmoid(g))).astype(o_ref.dtype)


def _dsa_attention(iq, iwt, ik, ckv, ckvt, qa, gate, wuvt, tbl, topk):
    b, s, _ = ckv.shape
    tq, tk = DSA_TQ, DSA_TK
    d_inner = N_HEADS * HEAD_DIM
    vrows = ckvt.shape[1]
    const = lambda *shape: pl.BlockSpec(shape, lambda bi, i: (0,) * len(shape))
    return pl.pallas_call(
        functools.partial(_dsa_attn_kernel, topk=topk),
        grid=(b, s // tq),
        in_specs=[pl.BlockSpec((1, IDX_HEADS, tq, IDX_DIM), lambda bi, i: (bi, 0, i, 0)),
                  pl.BlockSpec((1, IDX_HEADS, tq), lambda bi, i: (bi, 0, i)),
                  pl.BlockSpec((1, s, IDX_DIM), lambda bi, i: (bi, 0, 0)),
                  pl.BlockSpec((1, s, KV_RANK), lambda bi, i: (bi, 0, 0)),
                  pl.BlockSpec((1, vrows, s), lambda bi, i: (bi, 0, 0)),
                  pl.BlockSpec((1, N_HEADS, tq, KV_RANK), lambda bi, i: (bi, 0, i, 0)),
                  pl.BlockSpec((1, tq, d_inner), lambda bi, i: (bi, i, 0)),
                  const(N_HEADS, HEAD_DIM, KV_RANK),
                  const(*tbl.shape)],
        out_specs=pl.BlockSpec((1, tq, d_inner), lambda bi, i: (bi, i, 0)),
        out_shape=jax.ShapeDtypeStruct((b, s, d_inner), jnp.bfloat16),
        scratch_shapes=[pltpu.VMEM((s, tq), jnp.int32),
                        pltpu.VMEM((vrows, N_HEADS * tq), jnp.float32),
                        pltpu.VMEM((1, N_HEADS * tq), jnp.float32)],
        compiler_params=_cparams("parallel", "arbitrary"),
        name="dsa_attention",
    )(iq, iwt, ik, ckv, ckvt, qa, gate, wuvt, tbl)


def _fox_decay_kernel(f_ref, fb_ref, place_ref, o_ref, carry_ref):
    j = pl.program_id(1)

    @pl.when(j == 0)
    def _():
        carry_ref[...] = jnp.zeros(carry_ref.shape, jnp.float32)

    x = f_ref[0][:, :N_HEADS] + fb_ref[...]
    lf = jnp.minimum(x, 0.0) - jnp.log(1.0 + jnp.exp(-jnp.abs(x)))
    chunk = x.shape[0]
    r = lax.broadcasted_iota(jnp.int32, (chunk, chunk), 0)
    c = lax.broadcasted_iota(jnp.int32, (chunk, chunk), 1)
    tri = jnp.where(c <= r, 1.0, 0.0).astype(jnp.bfloat16)
    cs = carry_ref[0:1, :N_HEADS]
    for piece in _split_bf16(lf, 3):
        cs = cs + jnp.dot(tri, piece, preferred_element_type=jnp.float32)
    carry_ref[0:1, :N_HEADS] = cs[chunk - 1:chunk, :]
    out = jnp.zeros(o_ref.shape[1:], jnp.float32)
    for ci, piece in enumerate(_split_bf16(cs * (-LOG2E), FOX_DK_PIECES)):
        out = out + jnp.dot(piece, place_ref[ci], preferred_element_type=jnp.float32)
    o_ref[0] = out.astype(o_ref.dtype)


def _fox_decay(f_pre, f_bias, chunk=512):
    b, s, fw = f_pre.shape
    chunk = min(chunk, s)
    width = N_HEADS // FOX_HG * LANES
    heads = np.arange(N_HEADS)
    place = np.zeros((FOX_DK_PIECES, N_HEADS, width), np.float32)
    for ci in range(FOX_DK_PIECES):
        place[ci, heads, heads // FOX_HG * LANES + heads % FOX_HG * FOX_DK_PIECES + ci] = 1.0
    return pl.pallas_call(
        _fox_decay_kernel,
        grid=(b, s // chunk),
        in_specs=[pl.BlockSpec((1, chunk, fw), lambda bi, j: (bi, j, 0)),
                  pl.BlockSpec((1, N_HEADS), lambda bi, j: (0, 0)),
                  pl.BlockSpec((FOX_DK_PIECES, N_HEADS, width), lambda bi, j: (0, 0, 0))],
        out_specs=pl.BlockSpec((1, chunk, width), lambda bi, j: (bi, j, 0)),
        out_shape=jax.ShapeDtypeStruct((b, s, width), jnp.bfloat16),
        scratch_shapes=[pltpu.VMEM((SUBLANES, LANES), jnp.float32)],
        compiler_params=_cparams("parallel", "arbitrary"),
        name="fox_decay",
    )(f_pre, f_bias.reshape(1, N_HEADS), jnp.asarray(place, jnp.bfloat16))


def _fox_attn_kernel(q_ref, k_ref, kx_ref, vt_ref, gate_ref, o_ref, m_ref, acc_ref, *, tq, tk):
    i = pl.program_id(2)
    m_ref[...] = jnp.full(m_ref.shape, MASKED, jnp.float32)
    acc_ref[...] = jnp.zeros(acc_ref.shape, jnp.float32)
    lane = lax.broadcasted_iota(jnp.int32, (tq, LANES), 1)
    qxs = [jnp.where((lane >= hh * FOX_DK_PIECES) & (lane < (hh + 1) * FOX_DK_PIECES), 1.0, 0.0
                     ).astype(jnp.bfloat16) for hh in range(FOX_HG)]
    ones = jnp.ones((BF16_ROWS, tk), jnp.bfloat16)

    def tile(kt, masked):
        k0 = pl.multiple_of(kt * tk, tk)

        def scores_t(hh):
            cols = slice(hh * HEAD_DIM, (hh + 1) * HEAD_DIM)
            qc = jnp.concatenate([q_ref[0, :, cols], qxs[hh]], axis=1)
            kc = jnp.concatenate([k_ref[0, pl.ds(k0, tk), cols],
                                  kx_ref[0, pl.ds(k0, tk), :]], axis=1)
            return lax.dot_general(kc, qc, _NT, preferred_element_type=jnp.float32)

        sts = [scores_t(0)]
        for hh in range(FOX_HG):
            if hh + 1 < FOX_HG:
                sts.append(scores_t(hh + 1))
            cols = slice(hh * HEAD_DIM, (hh + 1) * HEAD_DIM)
            s = sts[hh]
            if masked:
                srow = lax.broadcasted_iota(jnp.int32, (tk, tq), 0)
                tcol = lax.broadcasted_iota(jnp.int32, (tk, tq), 1)
                s = jnp.where(srow <= tcol, s, MASKED)
            m_prev = m_ref[hh]
            m_new = jnp.maximum(m_prev, _colmax(s))
            alpha = jnp.exp2(m_prev - m_new)
            p = jnp.exp2(s - m_new).astype(jnp.bfloat16)
            m_ref[hh] = m_new
            vt = jnp.concatenate([vt_ref[0, cols, pl.ds(k0, tk)], ones], axis=0)
            acc_ref[hh] = acc_ref[hh] * alpha + jnp.dot(vt, p, preferred_element_type=jnp.float32)

    def full_tile(kt, carry):
        tile(kt, False)
        return carry

    lax.fori_loop(0, i, full_tile, 0)
    tile(i, True)

    for hh in range(FOX_HG):
        cols = slice(hh * HEAD_DIM, (hh + 1) * HEAD_DIM)
        a = acc_ref[hh]
        o = (a[:HEAD_DIM] / a[HEAD_DIM:HEAD_DIM + 1]).T
        g = gate_ref[0, :, cols].astype(jnp.float32)
        o_ref[0, :, cols] = (o * (g * jax.nn.sigmoid(g))).astype(o_ref.dtype)


def _fox_attention(proj, kx, vt, d_inner):
    b, s, _ = proj.shape
    tq = tk = min(FOX_TQ, s)
    nh = d_inner // HEAD_DIM
    ng = nh // FOX_HG
    w = FOX_HG * HEAD_DIM
    return pl.pallas_call(
        functools.partial(_fox_attn_kernel, tq=tq, tk=tk),
        grid=(b, ng, s // tq),
        in_specs=[pl.BlockSpec((1, tq, w), lambda bi, h, i: (bi, i, h)),
                  pl.BlockSpec((1, s, w), lambda bi, h, i: (bi, 0, ng + h)),
                  pl.BlockSpec((1, s, LANES), lambda bi, h, i: (bi, 0, h)),
                  pl.BlockSpec((1, w, s), lambda bi, h, i: (bi, h, 0)),
                  pl.BlockSpec((1, tq, w), lambda bi, h, i: (bi, i, 2 * ng + h))],
        out_specs=pl.BlockSpec((1, tq, w), lambda bi, h, i: (bi, i, h)),
        out_shape=jax.ShapeDtypeStruct((b, s, d_inner), jnp.bfloat16),
        scratch_shapes=[pltpu.VMEM((FOX_HG, 1, tq), jnp.float32),
                        pltpu.VMEM((FOX_HG, HEAD_DIM + BF16_ROWS, tq), jnp.float32)],
        compiler_params=_cparams("parallel", "parallel", "arbitrary"),
        name="fox_attention",
    )(proj, proj, kx, vt, proj)


def _dsa_layer(hn, h, g_next, hn_dtype, b, s, w_in, q_norm, kv_norm, w_q_up, w_uk, w_uv, w_iq, w_out, tbl):
    bf = jnp.bfloat16
    d_inner = N_HEADS * HEAD_DIM
    n_small = Q_RANK + KV_RANK + IDX_DIM + IDX_HEADS
    n_pad = -(-n_small // LANES) * LANES
    w_small = jnp.pad(w_in[:, :n_small], ((0, 0), (0, n_pad - n_small))).astype(bf)
    w_gate = w_in[:, n_small:].astype(bf)
    small = _matmul(hn, w_small, jnp.ones((n_pad,), jnp.float32), jnp.float32, 1024, n_pad)
    gate = _matmul(hn, w_gate, jnp.ones((d_inner,), jnp.float32), bf, 1024, 1024)
    wukt = jnp.transpose(w_uk, (1, 2, 0)).astype(bf)
    wuvt = jnp.transpose(w_uv, (1, 2, 0)).astype(bf)
    ckv, ckvt, ik, iw, qa, iq = _dsa_latents(small.reshape(b, s, n_pad), q_norm, kv_norm,
                                             w_q_up.astype(bf), wukt, w_iq.astype(bf),
                                             tm=min(512, s))
    iwt = jnp.transpose(iw, (0, 2, 1))
    topk = min(TOPK_MAX, s // 4)
    og = _dsa_attention(iq, iwt, ik, ckv, ckvt, qa, gate.reshape(b, s, d_inner), wuvt, tbl, topk)
    return _outproj(og.reshape(b * s, d_inner), w_out.astype(bf), h, g_next, hn_dtype)


def _fox_layer(hn, h, g_next, hn_dtype, b, s, w_in, f_bias, w_out):
    bf = jnp.bfloat16
    d_inner = N_HEADS * HEAD_DIM
    w_qkg = jnp.concatenate([w_in[:, :2 * d_inner], w_in[:, 3 * d_inner + N_HEADS:]], axis=1).astype(bf)
    w_vt = jnp.transpose(w_in[:, 2 * d_inner:3 * d_inner]).astype(bf)
    w_f = jnp.pad(w_in[:, 3 * d_inner:3 * d_inner + N_HEADS], ((0, 0), (0, LANES - N_HEADS))).astype(bf)
    colscale = jnp.concatenate([jnp.full((d_inner,), HEAD_DIM ** -0.5 * LOG2E, jnp.float32),
                                jnp.ones((2 * d_inner,), jnp.float32)])
    proj = _matmul(hn, w_qkg, colscale, bf, 1024, 1024)
    vt = _matmul_t(hn.reshape(b, s, -1), w_vt, bf, 1024, 1024)
    f_pre = _matmul(hn, w_f, jnp.ones((LANES,), jnp.float32), jnp.float32, 1024, LANES)
    kx = _fox_decay(f_pre.reshape(b, s, LANES), f_bias)
    og = _fox_attention(proj.reshape(b, s, 3 * d_inner), kx, vt, d_inner)
    return _outproj(og.reshape(b * s, d_inner), w_out.astype(bf), h, g_next, hn_dtype)


def kernel(x, norm_g, final_g, rel_bias, a_w_in, a_q_norm, a_kv_norm, a_w_q_up, a_w_uk, a_w_uv,
           a_w_iq, a_w_out, b_w_in, b_f_bias, b_w_out):
    b, s, d = x.shape
    depth = norm_g.shape[0]
    tbl = _bias_tables(rel_bias, DSA_TQ, DSA_TK)
    h = x.reshape(b * s, d)
    hn = _rmsnorm(h, norm_g[0], jnp.bfloat16)
    for i in range(depth):
        last = i == depth - 1
        g_next = final_g if last else norm_g[i + 1]
        hn_dtype = x.dtype if last else jnp.bfloat16
        j = i // 2
        if i % 2 == 0:
            h, hn = _dsa_layer(hn, h, g_next, hn_dtype, b, s, a_w_in[j], a_q_norm[j], a_kv_norm[j],
                               a_w_q_up[j], a_w_uk[j], a_w_uv[j], a_w_iq[j], a_w_out[j], tbl)
        else:
            h, hn = _fox_layer(hn, h, g_next, hn_dtype, b, s, b_w_in[j], b_f_bias[j], b_w_out[j])
    return hn.reshape(b, s, d)
```
